```python
import jax
import jax.numpy as jnp
from jax import lax
import numpy as np

D_MODEL = 1024
BATCH = 2
SEQ = 8192
DEPTH = 4
DEC_BATCH = 128
DEC_SEQ = 1
PAST_LEN = 2048
PAGE_SIZE = 128

HEAD_DIM = 64
N_HEADS = D_MODEL // HEAD_DIM
H_SB = N_HEADS // 2
H_MOBA = N_HEADS - H_SB
ROT_DIM = HEAD_DIM // 4
ROPE_THETA = 500000.0
Q_BLOCK = 128
MOBA_BLOCK = 256
MOBA_TOPK = 3
MOBA_Q_BLOCK = 64
D_FF = ((8 * D_MODEL // 3 + 127) // 128) * 128
CONV_WIDTH = 3
RMS_EPS = 1e-6
N_AB_LAYERS = (DEPTH + 1) // 2
N_C_LAYERS = DEPTH // 2
ATTN_SCALE = HEAD_DIM ** -0.5

kernel_name = "hybrid_stickbreak_moba_fox_convffn_step"


def _rmsnorm(x, g):
    xf = x.astype(jnp.float32)
    y = xf * lax.rsqrt(jnp.mean(xf * xf, axis=-1, keepdims=True) + RMS_EPS)
    return (y * g.astype(jnp.float32)).astype(x.dtype)


def _rope_partial(x, pos):
    half = ROT_DIM // 2
    inv = ROPE_THETA ** (-jnp.arange(half, dtype=jnp.float32) * 2.0 / ROT_DIM)
    ang = pos.astype(jnp.float32)[:, None] * inv[None, :]
    cos = jnp.cos(ang)[None, :, None, :]
    sin = jnp.sin(ang)[None, :, None, :]
    xf = x.astype(jnp.float32)
    x1 = xf[..., :half]
    x2 = xf[..., half:ROT_DIM]
    out = jnp.concatenate([x1 * cos - x2 * sin, x1 * sin + x2 * cos, xf[..., ROT_DIM:]], axis=-1)
    return out.astype(x.dtype)


def _to_blocks(a, blk):
    b, t = a.shape[:2]
    return jnp.moveaxis(a.reshape((b, t // blk, blk) + a.shape[2:]), 1, 0)


def _from_blocks(a):
    a = jnp.moveaxis(a, 0, 1)
    return a.reshape((a.shape[0], a.shape[1] * a.shape[2]) + a.shape[3:])


def _query_sweep(fn, blk, q_pos, *q_args):
    t = q_pos.shape[0]
    if t <= blk or t % blk:
        return fn(q_pos, *q_args)
    qp = q_pos.reshape(t // blk, blk)
    blocks = tuple(_to_blocks(a, blk) for a in q_args)
    out = lax.map(lambda a: fn(*a), (qp,) + blocks)
    return _from_blocks(out)


def _page_gather(pool, page_table):
    g = pool[page_table]
    return g.reshape((g.shape[0], g.shape[1] * g.shape[2]) + g.shape[3:])


def _sb_core(q_pos, q, k, v):
    tk = k.shape[1]
    z = jnp.einsum('bqhd,bkhd->bhqk', q, k).astype(jnp.float32) * ATTN_SCALE
    causal = jnp.arange(tk)[None, :] < q_pos[:, None]
    log_1m = jnp.where(causal, jax.nn.log_sigmoid(-z), 0.0)
    acc = lax.cumsum(log_1m, axis=3, reverse=True) - log_1m
    w = jnp.where(causal, jnp.exp(jax.nn.log_sigmoid(z) + acc), 0.0)
    return jnp.einsum('bhqk,bkhd->bqhd', w.astype(v.dtype), v)


def _moba_core(q_pos, q, kb, vb, k_mean, topk):
    b, h, nb, bs, _ = kb.shape
    tq = q.shape[1]
    q_blk = q_pos // MOBA_BLOCK
    gate = jnp.einsum('bqhd,bhnd->bhqn', q.astype(jnp.float32), k_mean)
    past = jnp.arange(nb)[None, :] < q_blk[:, None]
    gate = jnp.where(past, gate, -jnp.inf)
    _, top_idx = lax.top_k(gate, topk)
    top_ok = top_idx < q_blk[None, None, :, None]
    own = jnp.broadcast_to(q_blk[None, None, :, None], (b, h, tq, 1)).astype(top_idx.dtype)
    sel = jnp.concatenate([top_idx, own], axis=-1)
    ok = jnp.concatenate([top_ok, jnp.ones(own.shape, dtype=bool)], axis=-1)
    b_i = jnp.arange(b)[:, None, None, None]
    h_i = jnp.arange(h)[None, :, None, None]
    kg = kb[b_i, h_i, sel]
    vg = vb[b_i, h_i, sel]
    k_pos = sel[..., None] * bs + jnp.arange(bs)
    mask = ok[..., None] & (k_pos <= q_pos[None, None, :, None, None])
    s = jnp.einsum('bqhd,bhqnkd->bhqnk', q, kg).astype(jnp.float32) * ATTN_SCALE
    s = jnp.where(mask, s, -jnp.inf)
    n_sel = sel.shape[-1]
    p = jax.nn.softmax(s.reshape(b, h, tq, n_sel * bs), axis=-1).reshape(s.shape)
    return jnp.einsum('bhqnk,bhqnkd->bqhd', p.astype(vg.dtype), vg)


def _moba(q, k, v, q_pos):
    b, tk, h, d = k.shape
    nb = -(-tk // MOBA_BLOCK)
    pad = nb * MOBA_BLOCK - tk

    def blocks(a):
        a = jnp.pad(a, ((0, 0), (0, pad), (0, 0), (0, 0)))
        return a.reshape(b, nb, MOBA_BLOCK, h, d).transpose(0, 3, 1, 2, 4)

    kb = blocks(k)
    vb = blocks(v)
    k_mean = jnp.mean(kb.astype(jnp.float32), axis=3)
    topk = min(MOBA_TOPK, nb)
    return _query_sweep(lambda qp, qb: _moba_core(qp, qb, kb, vb, k_mean, topk), MOBA_Q_BLOCK, q_pos, q)


def _fox_core(q_pos, q, c_q, k, v, c_k):
    tk = k.shape[1]
    s = jnp.einsum('bqhd,bkhd->bhqk', q, k).astype(jnp.float32) * ATTN_SCALE
    s = s + jnp.transpose(c_q, (0, 2, 1))[:, :, :, None] - jnp.transpose(c_k, (0, 2, 1))[:, :, None, :]
    mask = jnp.arange(tk)[None, :] <= q_pos[:, None]
    p = jax.nn.softmax(jnp.where(mask, s, -jnp.inf), axis=-1)
    return jnp.einsum('bhqk,bkhd->bqhd', p.astype(v.dtype), v)


def _mixer_ab(h, pos, w_in, w_o, past_k=None, past_v=None):
    b, t, _ = h.shape
    qkv = (h @ w_in).reshape(b, t, 3, N_HEADS, HEAD_DIM)
    q, k, v = qkv[:, :, 0], qkv[:, :, 1], qkv[:, :, 2]
    q_mb = _rope_partial(q[:, :, H_SB:], pos)
    k_new = jnp.concatenate([k[:, :, :H_SB], _rope_partial(k[:, :, H_SB:], pos)], axis=2)
    if past_k is None:
        k_all, v_all = k_new, v
    else:
        k_all = jnp.concatenate([past_k.astype(k_new.dtype), k_new], axis=1)
        v_all = jnp.concatenate([past_v.astype(v.dtype), v], axis=1)
    k_sb, v_sb = k_all[:, :, :H_SB], v_all[:, :, :H_SB]
    o_sb = _query_sweep(lambda qp, qb: _sb_core(qp, qb, k_sb, v_sb), Q_BLOCK, pos, q[:, :, :H_SB])
    o_mb = _moba(q_mb, k_all[:, :, H_SB:], v_all[:, :, H_SB:], pos)
    o = jnp.concatenate([o_sb, o_mb], axis=2).reshape(b, t, D_MODEL)
    return o @ w_o, k_new, v


def _mixer_c(h, pos, w_in, b_f, w_o, past_k=None, past_v=None, past_logf=None):
    b, t, _ = h.shape
    proj = h @ w_in
    qkv = proj[..., :3 * D_MODEL].reshape(b, t, 3, N_HEADS, HEAD_DIM)
    q, k, v = qkv[:, :, 0], qkv[:, :, 1], qkv[:, :, 2]
    logf = jax.nn.log_sigmoid((proj[..., 3 * D_MODEL:] + b_f).astype(jnp.float32))
    if past_k is None:
        k_all, v_all, logf_all = k, v, logf
    else:
        k_all = jnp.concatenate([past_k.astype(k.dtype), k], axis=1)
        v_all = jnp.concatenate([past_v.astype(v.dtype), v], axis=1)
        logf_all = jnp.concatenate([past_logf.astype(jnp.float32), logf], axis=1)
    c_all = jnp.cumsum(logf_all, axis=1)
    c_q = c_all[:, c_all.shape[1] - t:]
    o = _query_sweep(lambda qp, qb, cb: _fox_core(qp, qb, cb, k_all, v_all, c_all), Q_BLOCK, pos, q, c_q)
    return o.reshape(b, t, D_MODEL) @ w_o, k, v, logf


def _conv_ffn(h, w_up, conv_w, conv_b, w_down, past_u):
    b, t, _ = h.shape
    u = h @ w_up
    if past_u is None:
        past_u = jnp.zeros((b, CONV_WIDTH - 1, u.shape[-1]), u.dtype)
    u_ext = jnp.concatenate([past_u.astype(u.dtype), u], axis=1)
    uc = conv_b
    for i in range(CONV_WIDTH):
        uc = uc + conv_w[i] * u_ext[:, i:i + t]
    gate, val = jnp.split(uc, 2, axis=-1)
    return (jax.nn.silu(gate) * val) @ w_down, u_ext[:, -(CONV_WIDTH - 1):]


def setup_inputs(seed: int = 0) -> dict:
    key = jax.random.key(seed)
    ks = jax.random.split(key, 22)
    f32 = jnp.float32
    n_pages = PAST_LEN // PAGE_SIZE
    n_used = DEC_BATCH * n_pages
    n_pool = n_used + n_used // 4
    d2f = 2 * D_FF
    s_in = D_MODEL ** -0.5

    def nrm(k, shape, scale):
        return jax.random.normal(k, shape, f32) * scale

    perm = jax.random.permutation(ks[0], n_pool)
    page_table = perm[:n_used].reshape(DEC_BATCH, n_pages).astype(jnp.int32)
    kv_ab = (N_AB_LAYERS, n_pool, PAGE_SIZE, N_HEADS, HEAD_DIM)
    kv_c = (N_C_LAYERS, n_pool, PAGE_SIZE, N_HEADS, HEAD_DIM)
    return {
        'x_prompt': nrm(ks[1], (BATCH, SEQ, D_MODEL), 1.0),
        'x_sample': nrm(ks[2], (DEC_BATCH, DEC_SEQ, D_MODEL), 1.0),
        'cache_k_ab': nrm(ks[3], kv_ab, 1.0),
        'cache_v_ab': nrm(ks[4], kv_ab, 1.0),
        'cache_k_c': nrm(ks[5], kv_c, 1.0),
        'cache_v_c': nrm(ks[6], kv_c, 1.0),
        'cache_logf_c': jax.nn.log_sigmoid(nrm(ks[7], (N_C_LAYERS, n_pool, PAGE_SIZE, N_HEADS), 1.0)),
        'state_conv': nrm(ks[8], (DEPTH, DEC_BATCH, CONV_WIDTH - 1, d2f), 1.0),
        'page_table': page_table,
        'w_in_ab': nrm(ks[9], (N_AB_LAYERS, D_MODEL, 3 * D_MODEL), s_in),
        'w_o_ab': nrm(ks[10], (N_AB_LAYERS, D_MODEL, D_MODEL), s_in),
        'w_in_c': nrm(ks[11], (N_C_LAYERS, D_MODEL, 3 * D_MODEL + N_HEADS), s_in),
        'b_f_c': nrm(ks[12], (N_C_LAYERS, N_HEADS), 0.1),
        'w_o_c': nrm(ks[13], (N_C_LAYERS, D_MODEL, D_MODEL), s_in),
        'g_mix': 1.0 + nrm(ks[14], (DEPTH, D_MODEL), 0.02),
        'g_ffn': 1.0 + nrm(ks[15], (DEPTH, D_MODEL), 0.02),
        'w_up': nrm(ks[16], (DEPTH, D_MODEL, d2f), s_in),
        'conv_w': nrm(ks[17], (DEPTH, CONV_WIDTH, d2f), CONV_WIDTH ** -0.5),
        'conv_b': nrm(ks[18], (DEPTH, d2f), 0.02),
        'w_down': nrm(ks[19], (DEPTH, D_FF, D_MODEL), D_FF ** -0.5),
        'g_final': 1.0 + nrm(ks[20], (D_MODEL,), 0.02),
    }


def reference(x_prompt, x_sample, cache_k_ab, cache_v_ab, cache_k_c, cache_v_c, cache_logf_c,
              state_conv, page_table, w_in_ab, w_o_ab, w_in_c, b_f_c, w_o_c, g_mix, g_ffn,
              w_up, conv_w, conv_b, w_down, g_final):
    past_len = page_table.shape[1] * cache_k_ab.shape[2]
    pos_p = jnp.arange(x_prompt.shape[1], dtype=jnp.int32)
    pos_s = past_len + jnp.arange(x_sample.shape[1], dtype=jnp.int32)
    xp, xs = x_prompt, x_sample
    k_ab_p, v_ab_p, k_ab_s, v_ab_s = [], [], [], []
    k_c_p, v_c_p, f_c_p, k_c_s, v_c_s, f_c_s = [], [], [], [], [], []
    conv_p, conv_s = [], []
    for l in range(DEPTH):
        hp = _rmsnorm(xp, g_mix[l])
        hs = _rmsnorm(xs, g_mix[l])
        i = l // 2
        if l % 2 == 0:
            pk = _page_gather(cache_k_ab[i], page_table)
            pv = _page_gather(cache_v_ab[i], page_table)
            op, kp, vp = _mixer_ab(hp, pos_p, w_in_ab[i], w_o_ab[i])
            os_, ks_, vs_ = _mixer_ab(hs, pos_s, w_in_ab[i], w_o_ab[i], pk, pv)
            k_ab_p.append(kp)
            v_ab_p.append(vp)
            k_ab_s.append(ks_)
            v_ab_s.append(vs_)
        else:
            pk = _page_gather(cache_k_c[i], page_table)
            pv = _page_gather(cache_v_c[i], page_table)
            pf = _page_gather(cache_logf_c[i], page_table)
            op, kp, vp, fp = _mixer_c(hp, pos_p, w_in_c[i], b_f_c[i], w_o_c[i])
            os_, ks_, vs_, fs_ = _mixer_c(hs, pos_s, w_in_c[i], b_f_c[i], w_o_c[i], pk, pv, pf)
            k_c_p.append(kp)
            v_c_p.append(vp)
            f_c_p.append(fp)
            k_c_s.append(ks_)
            v_c_s.append(vs_)
            f_c_s.append(fs_)
        xp = xp + op
        xs = xs + os_
        yp, up = _conv_ffn(_rmsnorm(xp, g_ffn[l]), w_up[l], conv_w[l], conv_b[l], w_down[l], None)
        ys, us = _conv_ffn(_rmsnorm(xs, g_ffn[l]), w_up[l], conv_w[l], conv_b[l], w_down[l], state_conv[l])
        xp = xp + yp
        xs = xs + ys
        conv_p.append(up)
        conv_s.append(us)
    y_prompt = _rmsnorm(xp, g_final)
    y_sample = _rmsnorm(xs, g_final)
    return (y_prompt, y_sample,
            jnp.stack(k_ab_p), jnp.stack(v_ab_p), jnp.stack(k_ab_s), jnp.stack(v_ab_s),
            jnp.stack(k_c_p), jnp.stack(v_c_p), jnp.stack(f_c_p),
            jnp.stack(k_c_s), jnp.stack(v_c_s), jnp.stack(f_c_s),
            jnp.stack(conv_p), jnp.stack(conv_s))
```

```python
import functools

import numpy as np
import jax
import jax.numpy as jnp
from jax import lax
from jax.experimental import pallas as pl
from jax.experimental.pallas import tpu as pltpu

F32 = jnp.float32
BF16 = jnp.bfloat16

HEAD_DIM = 64
N_HEADS = 16
H_SB = N_HEADS // 2
ROT_DIM = HEAD_DIM // 4
ROPE_THETA = 500000.0
MOBA_BLOCK = 256
MOBA_TOPK = 3
RMS_EPS = 1e-6
ATTN_SCALE = HEAD_DIM ** -0.5
CONV_WIDTH = 3
LOG2E = 1.4426950408889634
Q_SCALE = ATTN_SCALE * LOG2E

LANE = 128
SUBLANE = 8
NEG = -1e30
PROJ_VMEM = 48 * 1024 * 1024
DEC_VMEM = 56 * 1024 * 1024

_NT = (((1,), (1,)), ((), ()))


def _params(n_axes, vmem=PROJ_VMEM):
    return pltpu.CompilerParams(dimension_semantics=("arbitrary",) * n_axes,
                                vmem_limit_bytes=vmem)


def _rms(x, g):
    ms = jnp.mean(x * x, axis=-1, keepdims=True)
    return x * lax.rsqrt(ms + RMS_EPS) * g


def _log_sigmoid(z):
    return jnp.minimum(z, 0.0) - jnp.log1p(jnp.exp(-jnp.abs(z)))


def _log2_sigmoid(z2):
    return jnp.minimum(z2, 0.0) - jnp.log2(1.0 + jnp.exp2(-jnp.abs(z2)))


def _split2(x):
    hi = x.astype(BF16)
    lo = (x - hi.astype(F32)).astype(BF16)
    return hi, lo


def _proj_kernel(*refs, rope, emit_f32, gate, out_scale):
    it = iter(refs)
    x_ref, g_ref, w_ref = next(it), next(it), next(it)
    if rope:
        c_ref, s1_ref, s2_ref = next(it), next(it), next(it)
    if gate:
        wf_ref, bf_ref = next(it), next(it)
    if emit_f32:
        of_ref = next(it)
    ob_ref = next(it)
    if gate:
        og_ref = next(it)

    xn = _rms(x_ref[...], g_ref[...]).astype(BF16)
    y = jnp.dot(xn, w_ref[...], preferred_element_type=F32)
    if rope:
        half = y.shape[1] // 2
        c, s1, s2 = c_ref[...], s1_ref[...], s2_ref[...]
        parts = [y[:, :half]]
        for ch in range(half // LANE):
            yc = y[:, half + ch * LANE: half + (ch + 1) * LANE]
            up = pltpu.roll(yc, LANE - ROT_DIM // 2, 1)
            dn = pltpu.roll(yc, ROT_DIM // 2, 1)
            parts.append(yc * c + up * s1 + dn * s2)
        y = jnp.concatenate(parts, axis=1)
    if emit_f32:
        of_ref[...] = y
    ob_ref[...] = (y * out_scale if out_scale != 1.0 else y).astype(BF16)
    if gate:
        f = jnp.dot(xn, wf_ref[...], preferred_element_type=F32) + bf_ref[...]
        og_ref[...] = _log_sigmoid(f)


def _proj(x, g, w, *, tm, rope_tabs=None, gate=None, emit_f32=True, out_scale=1.0):
    m, d = x.shape
    n = w.shape[1]
    row = lambda i: (i, 0)
    fixed = lambda i: (0, 0)
    in_specs = [pl.BlockSpec((tm, d), row), pl.BlockSpec((1, d), fixed), pl.BlockSpec((d, n), fixed)]
    args = [x, g, w]
    if rope_tabs is not None:
        n_tab_blocks = rope_tabs[0].shape[0] // tm
        tab = lambda i: (i % n_tab_blocks, 0)
        in_specs += [pl.BlockSpec((tm, LANE), tab)] * 3
        args += list(rope_tabs)
    if gate is not None:
        in_specs += [pl.BlockSpec((d, LANE), fixed), pl.BlockSpec((1, LANE), fixed)]
        args += list(gate)
    out_shape, out_specs = [], []
    if emit_f32:
        out_shape.append(jax.ShapeDtypeStruct((m, n), F32))
        out_specs.append(pl.BlockSpec((tm, n), row))
    out_shape.append(jax.ShapeDtypeStruct((m, n), BF16))
    out_specs.append(pl.BlockSpec((tm, n), row))
    if gate is not None:
        out_shape.append(jax.ShapeDtypeStruct((m, LANE), F32))
        out_specs.append(pl.BlockSpec((tm, LANE), row))
    kern = functools.partial(_proj_kernel, rope=rope_tabs is not None, emit_f32=emit_f32,
                             gate=gate is not None, out_scale=out_scale)
    return pl.pallas_call(kern, grid=(m // tm,), in_specs=in_specs, out_specs=out_specs,
                          out_shape=out_shape, compiler_params=_params(1), name="proj")(*args)


def _oproj_kernel(x_ref, *refs, n_parts):
    o_refs, w_refs, out_ref = refs[:n_parts], refs[n_parts:2 * n_parts], refs[2 * n_parts]
    y = jnp.dot(o_refs[0][...], w_refs[0][...], preferred_element_type=F32)
    for o_ref, w_ref in zip(o_refs[1:], w_refs[1:]):
        y = y + jnp.dot(o_ref[...], w_ref[...], preferred_element_type=F32)
    out_ref[...] = x_ref[...] + y


def _oproj(x, parts, weights, *, tm):
    m, d = x.shape
    row = lambda i: (i, 0)
    fixed = lambda i: (0, 0)
    in_specs = [pl.BlockSpec((tm, d), row)]
    in_specs += [pl.BlockSpec((tm, p.shape[1]), row) for p in parts]
    in_specs += [pl.BlockSpec(w.shape, fixed) for w in weights]
    return pl.pallas_call(
        functools.partial(_oproj_kernel, n_parts=len(parts)), grid=(m // tm,),
        in_specs=in_specs, out_specs=pl.BlockSpec((tm, d), row),
        out_shape=jax.ShapeDtypeStruct((m, d), F32), compiler_params=_params(1),
        name="oproj")(x, *parts, *weights)


def _silu_gate(gt, vl):
    return gt * (1.0 / (1.0 + jnp.exp(-gt))) * vl


def _ffn_prompt_kernel(x_ref, g_ref, wg_ref, wv_ref, cwg_ref, cwv_ref, cbg_ref, cbv_ref, wd_ref,
                       out_ref, sg_ref, sv_ref, xn_sc, acc_sc, ug_sc, uv_sc, cg_sc, cv_sc):
    i = pl.program_id(1)
    c = pl.program_id(2)
    tm = x_ref.shape[0]

    @pl.when(c == 0)
    def _():
        x = x_ref[...]
        xn_sc[...] = _rms(x, g_ref[...]).astype(BF16)
        acc_sc[...] = x

    xn = xn_sc[...]

    def conv(w_ref, cw_ref, cb_ref, u_sc, carry_sc, s_ref):
        u = jnp.dot(xn, w_ref[...], preferred_element_type=F32)

        @pl.when(i == 0)
        def _():
            u_sc[0:SUBLANE, :] = jnp.zeros((SUBLANE, u.shape[1]), F32)

        @pl.when(i != 0)
        def _():
            u_sc[0:SUBLANE, :] = carry_sc[c]

        u_sc[SUBLANE:SUBLANE + tm, :] = u
        tail = u[tm - SUBLANE:tm, :]
        carry_sc[c] = tail
        s_ref[...] = tail
        cw = cw_ref[...]
        u0 = u_sc[SUBLANE - 2:SUBLANE - 2 + tm, :]
        u1 = u_sc[SUBLANE - 1:SUBLANE - 1 + tm, :]
        return ((cb_ref[...] + cw[0:1] * u0) + cw[1:2] * u1) + cw[2:3] * u

    gt = conv(wg_ref, cwg_ref, cbg_ref, ug_sc, cg_sc, sg_ref)
    vl = conv(wv_ref, cwv_ref, cbv_ref, uv_sc, cv_sc, sv_ref)
    act = _silu_gate(gt, vl).astype(BF16)
    acc_sc[...] += jnp.dot(act, wd_ref[...], preferred_element_type=F32)

    @pl.when(c == pl.num_programs(2) - 1)
    def _():
        out_ref[...] = acc_sc[...]


def _ffn_prompt(x, g, w_up, conv_w, conv_b, w_down, *, tm, cw):
    b, t, d = x.shape
    dff = w_down.shape[0]
    nc = dff // cw
    xmap = lambda bi, i, c: (bi, i, 0)
    in_specs = [
        pl.BlockSpec((None, tm, d), xmap),
        pl.BlockSpec((1, d), lambda bi, i, c: (0, 0)),
        pl.BlockSpec((d, cw), lambda bi, i, c: (0, c)),
        pl.BlockSpec((d, cw), lambda bi, i, c: (0, c + nc)),
        pl.BlockSpec((CONV_WIDTH, cw), lambda bi, i, c: (0, c)),
        pl.BlockSpec((CONV_WIDTH, cw), lambda bi, i, c: (0, c + nc)),
        pl.BlockSpec((1, cw), lambda bi, i, c: (0, c)),
        pl.BlockSpec((1, cw), lambda bi, i, c: (0, c + nc)),
        pl.BlockSpec((cw, d), lambda bi, i, c: (c, 0)),
    ]
    smap = lambda bi, i, c: (bi, i, 0, c)
    out_specs = [pl.BlockSpec((None, tm, d), xmap),
                 pl.BlockSpec((None, None, SUBLANE, cw), smap),
                 pl.BlockSpec((None, None, SUBLANE, cw), smap)]
    out_shape = [jax.ShapeDtypeStruct((b, t, d), F32),
                 jax.ShapeDtypeStruct((b, t // tm, SUBLANE, dff), F32),
                 jax.ShapeDtypeStruct((b, t // tm, SUBLANE, dff), F32)]
    scratch = [pltpu.VMEM((tm, d), BF16), pltpu.VMEM((tm, d), F32),
               pltpu.VMEM((tm + SUBLANE, cw), F32), pltpu.VMEM((tm + SUBLANE, cw), F32),
               pltpu.VMEM((nc, SUBLANE, cw), F32), pltpu.VMEM((nc, SUBLANE, cw), F32)]
    y, sg, sv = pl.pallas_call(
        _ffn_prompt_kernel, grid=(b, t // tm, nc), in_specs=in_specs, out_specs=out_specs,
        out_shape=out_shape, scratch_shapes=scratch, compiler_params=_params(3),
        name="ffn_prompt")(x, g, w_up, w_up, conv_w, conv_w, conv_b, conv_b, w_down)
    keep = CONV_WIDTH - 1
    state = jnp.concatenate([sg[:, -1, SUBLANE - keep:, :], sv[:, -1, SUBLANE - keep:, :]], axis=-1)
    return y, state


def _ffn_sample_kernel(x_ref, g_ref, wg_ref, wv_ref, cwg_ref, cwv_ref, cbg_ref, cbv_ref, wd_ref,
                       s0g_ref, s0v_ref, s1g_ref, s1v_ref, out_ref, ug_ref, uv_ref, xn_sc, acc_sc):
    c = pl.program_id(0)

    @pl.when(c == 0)
    def _():
        x = x_ref[...]
        xn_sc[...] = _rms(x, g_ref[...]).astype(BF16)
        acc_sc[...] = x

    xn = xn_sc[...]

    def conv(w_ref, cw_ref, cb_ref, s0_ref, s1_ref, u_ref):
        u = jnp.dot(xn, w_ref[...], preferred_element_type=F32)
        u_ref[...] = u
        cw = cw_ref[...]
        return ((cb_ref[...] + cw[0:1] * s0_ref[...]) + cw[1:2] * s1_ref[...]) + cw[2:3] * u

    gt = conv(wg_ref, cwg_ref, cbg_ref, s0g_ref, s1g_ref, ug_ref)
    vl = conv(wv_ref, cwv_ref, cbv_ref, s0v_ref, s1v_ref, uv_ref)
    act = _silu_gate(gt, vl).astype(BF16)
    acc_sc[...] += jnp.dot(act, wd_ref[...], preferred_element_type=F32)

    @pl.when(c == pl.num_programs(0) - 1)
    def _():
        out_ref[...] = acc_sc[...]


def _ffn_sample(x, g, w_up, conv_w, conv_b, w_down, state, *, cw):
    m, d = x.shape
    dff = w_down.shape[0]
    nc = dff // cw
    st = state.reshape(m, (CONV_WIDTH - 1) * 2 * dff)
    fixed = lambda c: (0, 0)
    in_specs = [
        pl.BlockSpec((m, d), fixed),
        pl.BlockSpec((1, d), fixed),
        pl.BlockSpec((d, cw), lambda c: (0, c)),
        pl.BlockSpec((d, cw), lambda c: (0, c + nc)),
        pl.BlockSpec((CONV_WIDTH, cw), lambda c: (0, c)),
        pl.BlockSpec((CONV_WIDTH, cw), lambda c: (0, c + nc)),
        pl.BlockSpec((1, cw), lambda c: (0, c)),
        pl.BlockSpec((1, cw), lambda c: (0, c + nc)),
        pl.BlockSpec((cw, d), lambda c: (c, 0)),
        pl.BlockSpec((m, cw), lambda c: (0, c)),
        pl.BlockSpec((m, cw), lambda c: (0, c + nc)),
        pl.BlockSpec((m, cw), lambda c: (0, c + 2 * nc)),
        pl.BlockSpec((m, cw), lambda c: (0, c + 3 * nc)),
    ]
    out_specs = [pl.BlockSpec((m, d), fixed),
                 pl.BlockSpec((m, cw), lambda c: (0, c)),
                 pl.BlockSpec((m, cw), lambda c: (0, c))]
    out_shape = [jax.ShapeDtypeStruct((m, d), F32),
                 jax.ShapeDtypeStruct((m, dff), F32),
                 jax.ShapeDtypeStruct((m, dff), F32)]
    scratch = [pltpu.VMEM((m, d), BF16), pltpu.VMEM((m, d), F32)]
    y, ug, uv = pl.pallas_call(
        _ffn_sample_kernel, grid=(nc,), in_specs=in_specs, out_specs=out_specs,
        out_shape=out_shape, scratch_shapes=scratch, compiler_params=_params(1),
        name="ffn_sample")(x, g, w_up, w_up, conv_w, conv_w, conv_b, conv_b, w_down, st, st, st, st)
    u = jnp.concatenate([ug, uv], axis=-1)
    new_state = jnp.stack([state[:, 1, :], u], axis=1)
    return y, new_state


def _norm_kernel(x_ref, g_ref, o_ref):
    o_ref[...] = _rms(x_ref[...], g_ref[...])


def _final_norm(x, g, *, tm):
    m, d = x.shape
    return pl.pallas_call(
        _norm_kernel, grid=(m // tm,),
        in_specs=[pl.BlockSpec((tm, d), lambda i: (i, 0)), pl.BlockSpec((1, d), lambda i: (0, 0))],
        out_specs=pl.BlockSpec((tm, d), lambda i: (i, 0)),
        out_shape=jax.ShapeDtypeStruct((m, d), F32), compiler_params=_params(1),
        name="final_norm")(x, g)


def _split3(x):
    hi = x.astype(BF16)
    r1 = x - hi.astype(F32)
    mid = r1.astype(BF16)
    lo = (r1 - mid.astype(F32)).astype(BF16)
    return hi, mid, lo


def _cumsum_kernel(x_ref, u_ref, hi_ref, mid_ref, lo_ref, carry_sc):
    @pl.when(pl.program_id(1) == 0)
    def _():
        carry_sc[...] = jnp.zeros_like(carry_sc)

    u = u_ref[...]
    hi, mid, lo = _split3(x_ref[...])
    c = (jnp.dot(hi, u, preferred_element_type=F32) + jnp.dot(mid, u, preferred_element_type=F32)
         + jnp.dot(lo, u, preferred_element_type=F32)) + carry_sc[...]
    carry_sc[...] = c[:, c.shape[1] - 1:]
    hi_ref[...], mid_ref[...], lo_ref[...] = _split3(c * (-LOG2E))


def _cumsum_time(logf_t, *, tc):
    b, h, t = logf_t.shape
    upper = jnp.asarray(np.triu(np.ones((tc, tc), np.float32)), BF16)
    spec = pl.BlockSpec((None, h, tc), lambda bi, i: (bi, 0, i))
    return pl.pallas_call(
        _cumsum_kernel, grid=(b, t // tc),
        in_specs=[spec, pl.BlockSpec((tc, tc), lambda bi, i: (0, 0))],
        out_specs=[spec] * 3,
        out_shape=[jax.ShapeDtypeStruct((b, h, t), BF16)] * 3,
        scratch_shapes=[pltpu.VMEM((h, 1), F32)], compiler_params=_params(2),
        name="cumsum")(logf_t, upper)


def _kmean_kernel(k_ref, o_ref):
    o_ref[...] = jnp.mean(k_ref[...], axis=0, keepdims=True)


def _kmean(k_flat, *, col_block, width):
    m = k_flat.shape[0]
    nblk = m // MOBA_BLOCK
    out = pl.pallas_call(
        _kmean_kernel, grid=(nblk,),
        in_specs=[pl.BlockSpec((MOBA_BLOCK, width), lambda n: (n, col_block))],
        out_specs=pl.BlockSpec((None, 1, width), lambda n: (n, 0, 0)),
        out_shape=jax.ShapeDtypeStruct((nblk, 1, width), F32), compiler_params=_params(1),
        name="kmean")(k_flat)
    return out.reshape(nblk, width)


def _stack_heads(q2, tq):
    lane = lax.broadcasted_iota(jnp.int32, (tq, LANE), 1)
    zero = jnp.zeros_like(q2)
    return jnp.where(lane < HEAD_DIM, q2, zero), jnp.where(lane >= HEAD_DIM, q2, zero)


def _unstack_heads(o, tq):
    lane = lax.broadcasted_iota(jnp.int32, (tq, LANE), 1)
    return jnp.where(lane < HEAD_DIM, o[:tq], o[tq:])


def _causal_keep(tq, tk, strict):
    r = lax.broadcasted_iota(jnp.int32, (2 * tq, tk), 0)
    r = jnp.where(r >= tq, r - tq, r)
    c = lax.broadcasted_iota(jnp.int32, (2 * tq, tk), 1)
    return (c < r) if strict else (c <= r)


def _softmax_step(s, v2, m_sc, acc_sc):
    tq = s.shape[0] // 2
    m_prev = m_sc[...]
    m_new = jnp.maximum(m_prev, jnp.max(s, axis=1, keepdims=True))
    alpha = jnp.exp2(m_prev - m_new)
    p = jnp.exp2(s - jnp.concatenate([m_new] * (s.shape[1] // LANE), axis=1)).astype(BF16)
    lane = lax.broadcasted_iota(jnp.int32, v2.shape, 1)
    one = jnp.ones_like(v2)
    pv = jnp.concatenate(
        [jnp.dot(p[:tq], jnp.where(lane < HEAD_DIM, v2, one), preferred_element_type=F32),
         jnp.dot(p[tq:], jnp.where(lane >= HEAD_DIM, v2, one), preferred_element_type=F32)], axis=0)
    acc_sc[...] = alpha * acc_sc[...] + pv
    m_sc[...] = m_new


def _softmax_finish(acc_sc, tq):
    acc = acc_sc[...]
    o = acc / pltpu.roll(acc, HEAD_DIM, 1)
    return _unstack_heads(o, tq).astype(BF16)


def _flash_loop(qi, tq, lhs_sc, k_ref, aug_ref, v_ref, m_sc, acc_sc):
    m_sc[...] = jnp.full(m_sc.shape, NEG, F32)
    acc_sc[...] = jnp.zeros(acc_sc.shape, F32)

    def chunk(j, diag):
        off = pl.multiple_of(j * tq, tq)
        kaug = jnp.concatenate([k_ref[pl.ds(off, tq), :], aug_ref[pl.ds(off, tq), :]], axis=1)
        s = lax.dot_general(lhs_sc[...], kaug, _NT, preferred_element_type=F32)
        if diag:
            s = jnp.where(_causal_keep(tq, tq, False), s, NEG)
        _softmax_step(s, v_ref[pl.ds(off, tq), :], m_sc, acc_sc)

    def body(j, carry):
        chunk(j, False)
        return carry

    lax.fori_loop(0, qi, body, 0)
    chunk(qi, True)


def _fox_kernel(q_ref, k_ref, v_ref, ka_ref, o_ref, lhs_sc, m_sc, acc_sc, *, tq):
    qa, qb = _stack_heads(q_ref[...], tq)
    lhs_sc[0:tq, 0:LANE] = qa
    lhs_sc[tq:2 * tq, 0:LANE] = qb
    lane = lax.broadcasted_iota(jnp.int32, (2 * tq, LANE), 1)
    row = lax.broadcasted_iota(jnp.int32, (2 * tq, LANE), 0)
    piece_head = jnp.where(lane < 3, 0, jnp.where(lane < 6, 1, 2))
    lhs_sc[:, LANE:2 * LANE] = jnp.where(piece_head == jnp.where(row >= tq, 1, 0), 1.0, 0.0).astype(BF16)
    _flash_loop(pl.program_id(2), tq, lhs_sc, k_ref, ka_ref, v_ref, m_sc, acc_sc)
    o_ref[...] = _softmax_finish(acc_sc, tq)


def _flash_scratch(tq):
    return [pltpu.VMEM((2 * tq, 2 * LANE), BF16), pltpu.VMEM((2 * tq, LANE), F32),
            pltpu.VMEM((2 * tq, LANE), F32)]


def _fox_attention(q, k, v, ka, *, b, t, tq):
    m = q.shape[0]
    n_hp = q.shape[1] // LANE
    nq = t // tq
    return pl.pallas_call(
        functools.partial(_fox_kernel, tq=tq), grid=(b, n_hp, nq),
        in_specs=[pl.BlockSpec((tq, LANE), lambda bi, hp, qi: (bi * nq + qi, hp)),
                  pl.BlockSpec((t, LANE), lambda bi, hp, qi: (bi, hp)),
                  pl.BlockSpec((t, LANE), lambda bi, hp, qi: (bi, hp)),
                  pl.BlockSpec((None, None, t, LANE), lambda bi, hp, qi: (bi, hp, 0, 0))],
        out_specs=pl.BlockSpec((tq, LANE), lambda bi, hp, qi: (bi * nq + qi, hp)),
        out_shape=jax.ShapeDtypeStruct((m, n_hp * LANE), BF16),
        scratch_shapes=_flash_scratch(tq), compiler_params=_params(3), name="fox_attn")(q, k, v, ka)


def _sb_kernel(q_ref, k_ref, v_ref, u_ref, o_ref, qs_sc, r_sc, acc_sc, *, tq, sub):
    qi = pl.program_id(2)
    qa, qb = _stack_heads(q_ref[...], tq)
    qs_sc[0:tq, :] = qa
    qs_sc[tq:2 * tq, :] = qb
    r_sc[...] = jnp.zeros(r_sc.shape, F32)
    acc_sc[...] = jnp.zeros(acc_sc.shape, F32)
    u = u_ref[...]

    def chunk(j, diag):
        off = pl.multiple_of(j * tq, tq)
        k2 = k_ref[pl.ds(off, tq), :]
        v2 = v_ref[pl.ds(off, tq), :]
        z_all = lax.dot_general(qs_sc[...], k2, _NT, preferred_element_type=F32)
        keep_all = _causal_keep(tq, tq, True) if diag else None
        for sb in reversed(range(tq // sub)):
            z = z_all[:, sb * sub:(sb + 1) * sub]
            sp = jnp.maximum(z, 0.0) + jnp.log2(1.0 + jnp.exp2(-jnp.abs(z)))
            if diag:
                keep = keep_all[:, sb * sub:(sb + 1) * sub]
                sp = jnp.where(keep, sp, 0.0)
            cum = jnp.dot(sp.astype(BF16), u, preferred_element_type=F32)
            r = r_sc[...]
            w = jnp.exp2((z - sp) + (cum + jnp.concatenate([r] * (sub // LANE), axis=1)))
            if diag:
                w = jnp.where(keep, w, 0.0)
            acc_sc[...] += jnp.dot(w.astype(BF16), v2[sb * sub:(sb + 1) * sub, :],
                                   preferred_element_type=F32)
            r_sc[...] = r + (cum[:, 0:1] - sp[:, 0:1])

    chunk(qi, True)

    def body(jj, carry):
        chunk(qi - 1 - jj, False)
        return carry

    lax.fori_loop(0, qi, body, 0)
    o_ref[...] = _unstack_heads(acc_sc[...], tq).astype(BF16)


def _sb_attention(q, k, v, *, b, t, tq, n_hp, sub=256):
    m = q.shape[0]
    nq = t // tq
    lower = jnp.asarray(-np.tril(np.ones((sub, sub), np.float32), -1), BF16)
    scratch = [pltpu.VMEM((2 * tq, LANE), BF16), pltpu.VMEM((2 * tq, LANE), F32),
               pltpu.VMEM((2 * tq, LANE), F32)]
    return pl.pallas_call(
        functools.partial(_sb_kernel, tq=tq, sub=sub), grid=(b, n_hp, nq),
        in_specs=[pl.BlockSpec((tq, LANE), lambda bi, hp, qi: (bi * nq + qi, hp)),
                  pl.BlockSpec((t, LANE), lambda bi, hp, qi: (bi, hp)),
                  pl.BlockSpec((t, LANE), lambda bi, hp, qi: (bi, hp)),
                  pl.BlockSpec((sub, sub), lambda bi, hp, qi: (0, 0))],
        out_specs=pl.BlockSpec((tq, LANE), lambda bi, hp, qi: (bi * nq + qi, hp)),
        out_shape=jax.ShapeDtypeStruct((m, n_hp * LANE), BF16),
        scratch_shapes=scratch, compiler_params=_params(3), name="sb_attn")(q, k, v, lower)


def _moba_kernel(q_ref, k_ref, v_ref, km_ref, et_ref, o_ref, lhs_sc, m_sc, acc_sc, *, tq):
    qi = pl.program_id(2)
    qa, qb = _stack_heads(q_ref[...], tq)
    lhs_sc[0:tq, 0:LANE] = qa
    lhs_sc[tq:2 * tq, 0:LANE] = qb

    qs = lhs_sc[:, 0:LANE]
    km_hi, km_lo = _split2(km_ref[...])
    gate = (lax.dot_general(qs, km_hi, _NT, preferred_element_type=F32)
            + lax.dot_general(qs, km_lo, _NT, preferred_element_type=F32))
    n_idx = lax.broadcasted_iota(jnp.int32, (2 * tq, LANE), 1)
    row = lax.broadcasted_iota(jnp.int32, (2 * tq, LANE), 0)
    pos = qi * tq + jnp.where(row >= tq, row - tq, row)
    q_blk = jnp.right_shift(pos, MOBA_BLOCK.bit_length() - 1)
    n_f = n_idx.astype(F32)
    g = jnp.where(n_idx < q_blk, gate, -jnp.inf)
    bias = jnp.where(n_idx == q_blk, 0.0, NEG)
    for _ in range(MOBA_TOPK):
        mx = jnp.max(g, axis=1, keepdims=True)
        first = jnp.min(jnp.where(g == mx, n_f, float(LANE)), axis=1, keepdims=True)
        hit = n_f == first
        bias = jnp.where(jnp.where(hit, n_idx, q_blk) < q_blk, 0.0, bias)
        g = jnp.where(hit, -jnp.inf, g)
    lhs_sc[:, LANE:2 * LANE] = bias.astype(BF16)
    _flash_loop(qi, tq, lhs_sc, k_ref, et_ref, v_ref, m_sc, acc_sc)
    o_ref[...] = _softmax_finish(acc_sc, tq)


def _moba_attention(q, k, v, kmean_pad, *, b, t, tq, n_hp, hp0):
    m = q.shape[0]
    nq = t // tq
    et = (np.arange(t)[:, None] // MOBA_BLOCK == np.arange(LANE)[None, :]).astype(np.float32)
    et = jnp.asarray(et, BF16)
    scratch = _flash_scratch(tq)
    return pl.pallas_call(
        functools.partial(_moba_kernel, tq=tq), grid=(b, n_hp, nq),
        in_specs=[pl.BlockSpec((tq, LANE), lambda bi, hp, qi: (bi * nq + qi, hp0 + hp)),
                  pl.BlockSpec((t, LANE), lambda bi, hp, qi: (bi, hp0 + hp)),
                  pl.BlockSpec((t, LANE), lambda bi, hp, qi: (bi, hp0 + hp)),
                  pl.BlockSpec((None, LANE, LANE), lambda bi, hp, qi: (bi, 0, hp)),
                  pl.BlockSpec((t, LANE), lambda bi, hp, qi: (0, 0))],
        out_specs=pl.BlockSpec((tq, LANE), lambda bi, hp, qi: (bi * nq + qi, hp)),
        out_shape=jax.ShapeDtypeStruct((m, n_hp * LANE), BF16),
        scratch_shapes=scratch, compiler_params=_params(3), name="moba_attn")(q, k, v, kmean_pad, et)


def _dec_scores(qb_ref, kt_refs, s_sc, page):
    for h in range(N_HEADS):
        qh = qb_ref[h]
        for p, kt_ref in enumerate(kt_refs):
            s_sc[h:h + 1, p * page:(p + 1) * page] = jnp.sum(kt_ref[h] * qh, axis=0, keepdims=True)


def _dec_new_score(qrow_ref, kn_ref, eh):
    prod = jnp.broadcast_to(kn_ref[...] * qrow_ref[...], (LANE, eh.shape[1]))
    hi, lo = _split2(prod)
    return (lax.dot_general(eh, hi, _NT, preferred_element_type=F32)
            + lax.dot_general(eh, lo, _NT, preferred_element_type=F32))


def _lane_suffix(x, tri):
    hi, lo = _split2(x)
    return jnp.dot(hi, tri, preferred_element_type=F32) + jnp.dot(lo, tri, preferred_element_type=F32)


def _dec_values(w_sc, vt_refs, acc_sc, page):
    for h in range(N_HEADS):
        acc = jnp.zeros(acc_sc.shape[1:], F32)
        for p, vt_ref in enumerate(vt_refs):
            acc = acc + vt_ref[h] * w_sc[h:h + 1, p * page:(p + 1) * page]
        acc_sc[h] = acc


def _dec_output(acc_sc, p_new, vn_ref, eh, o_ref):
    d = eh.shape[1]
    hi, lo = _split2(acc_sc[...].reshape(d, LANE))
    ones = jnp.ones((SUBLANE, LANE), BF16)
    o = (lax.dot_general(ones, hi, _NT, preferred_element_type=F32)
         + lax.dot_general(ones, lo, _NT, preferred_element_type=F32))[0:1]
    pn = jnp.sum(jnp.concatenate([p_new] * (d // LANE), axis=1) * eh.astype(F32), axis=0, keepdims=True)
    o_ref[...] = (o + pn * vn_ref[...]).astype(BF16)


def _dec_ab_kernel(pt_ref, qb_ref, qrow_ref, kn_ref, vn_ref, eh_ref, tri_ref, *rest, n_pages, page):
    kt_refs, vt_refs = rest[:n_pages], rest[n_pages:2 * n_pages]
    o_ref, s_sc, w_sc, acc_sc = rest[2 * n_pages:2 * n_pages + 4]
    eh, tri = eh_ref[...], tri_ref[...]
    lane_f = lax.broadcasted_iota(jnp.int32, (N_HEADS, LANE), 1).astype(F32)
    is_sb = lax.broadcasted_iota(jnp.int32, (N_HEADS, LANE), 0) < H_SB

    _dec_scores(qb_ref, kt_refs, s_sc, page)
    s_new = _dec_new_score(qrow_ref, kn_ref, eh)

    ppb = MOBA_BLOCK // page
    nblk = n_pages // ppb
    g = jnp.full((N_HEADS, LANE), -jnp.inf, F32)
    for n in range(nblk):
        gn = jnp.sum(s_sc[:, n * MOBA_BLOCK:(n + 1) * MOBA_BLOCK], axis=1, keepdims=True)
        g = jnp.where(lane_f == float(n), gn, g)
    firsts = []
    for _ in range(min(MOBA_TOPK, nblk + 1)):
        mx = jnp.max(g, axis=1, keepdims=True)
        first = jnp.min(jnp.where(g == mx, lane_f, float(LANE)), axis=1, keepdims=True)
        firsts.append(first)
        g = jnp.where(lane_f == first, -jnp.inf, g)
    sel_bias = []
    for n in range(nblk):
        miss = functools.reduce(jnp.minimum, [jnp.abs(f - float(n)) for f in firsts])
        sel_bias.append(jnp.where(miss == 0.0, 0.0, NEG))
    m = s_new
    for p in range(n_pages):
        sc = s_sc[:, p * page:(p + 1) * page] + sel_bias[p // ppb]
        m = jnp.maximum(m, jnp.max(sc, axis=1, keepdims=True))

    carry = jnp.zeros((N_HEADS, LANE), F32)
    lsum = jnp.zeros((N_HEADS, LANE), F32)
    for p in reversed(range(n_pages)):
        z = s_sc[:, p * page:(p + 1) * page]
        ls = _log2_sigmoid(z)
        l1m = ls - z
        cum = _lane_suffix(l1m, tri)
        w_sb = jnp.exp2(ls + (cum + carry))
        carry = carry + (cum[:, 0:1] + l1m[:, 0:1])
        pm = jnp.exp2((z + sel_bias[p // ppb]) - m)
        lsum = lsum + pm
        w_sc[:, p * page:(p + 1) * page] = jnp.where(is_sb, w_sb, pm)
    p_new = jnp.where(is_sb, 0.0, jnp.exp2(s_new - m))
    scale = jnp.where(is_sb, 1.0, 1.0 / (jnp.sum(lsum, axis=1, keepdims=True) + p_new))
    for p in range(n_pages):
        w_sc[:, p * page:(p + 1) * page] = w_sc[:, p * page:(p + 1) * page] * scale
    _dec_values(w_sc, vt_refs, acc_sc, page)
    _dec_output(acc_sc, p_new * scale, vn_ref, eh, o_ref)


def _dec_c_kernel(pt_ref, qb_ref, qrow_ref, kn_ref, vn_ref, lfn_ref, eh_ref, tri_ref, *rest,
                  n_pages, page):
    kt_refs, vt_refs = rest[:n_pages], rest[n_pages:2 * n_pages]
    lf_refs = rest[2 * n_pages:3 * n_pages]
    o_ref, s_sc, w_sc, acc_sc = rest[3 * n_pages:3 * n_pages + 4]
    eh, tri = eh_ref[...], tri_ref[...]

    _dec_scores(qb_ref, kt_refs, s_sc, page)
    s_new = _dec_new_score(qrow_ref, kn_ref, eh)

    carry = lfn_ref[...] * LOG2E
    m = s_new
    for p in reversed(range(n_pages)):
        lf = lf_refs[p][...] * LOG2E
        cum = _lane_suffix(lf, tri)
        sc = s_sc[:, p * page:(p + 1) * page] + (cum + carry)
        s_sc[:, p * page:(p + 1) * page] = sc
        carry = carry + (cum[:, 0:1] + lf[:, 0:1])
        m = jnp.maximum(m, jnp.max(sc, axis=1, keepdims=True))

    lsum = jnp.zeros((N_HEADS, LANE), F32)
    for p in range(n_pages):
        pm = jnp.exp2(s_sc[:, p * page:(p + 1) * page] - m)
        lsum = lsum + pm
        w_sc[:, p * page:(p + 1) * page] = pm
    p_new = jnp.exp2(s_new - m)
    scale = 1.0 / (jnp.sum(lsum, axis=1, keepdims=True) + p_new)
    for p in range(n_pages):
        w_sc[:, p * page:(p + 1) * page] = w_sc[:, p * page:(p + 1) * page] * scale
    _dec_values(w_sc, vt_refs, acc_sc, page)
    _dec_output(acc_sc, p_new * scale, vn_ref, eh, o_ref)


def _decode_attention(layer, page_table, q, kn, vn, cache_k, cache_v, cache_lf=None, lfn=None):
    nb, d = q.shape
    n_pages = page_table.shape[1]
    page = cache_k.shape[2]
    assert page == LANE
    kt = jnp.transpose(cache_k, (0, 1, 3, 4, 2))
    vt = jnp.transpose(cache_v, (0, 1, 3, 4, 2))
    col_head = np.arange(d) // HEAD_DIM
    eh = jnp.asarray((np.arange(N_HEADS)[:, None] == col_head[None, :]).astype(np.float32), BF16)
    tri = jnp.asarray(np.tril(np.ones((page, page), np.float32), -1), BF16)
    qb = jnp.broadcast_to(q.reshape(nb, N_HEADS, HEAD_DIM, 1), (nb, N_HEADS, HEAD_DIM, LANE))
    row3 = lambda a: a.reshape(nb, 1, a.shape[1])
    row_spec = pl.BlockSpec((None, 1, d), lambda bi, pt: (bi, 0, 0))
    fixed = lambda shape: pl.BlockSpec(shape, lambda bi, pt: (0, 0))

    def page_spec(p):
        return pl.BlockSpec((None, None, N_HEADS, HEAD_DIM, page),
                            lambda bi, pt: (layer, pt[bi, p], 0, 0, 0))

    def lf_spec(p):
        return pl.BlockSpec((None, None, N_HEADS, page), lambda bi, pt: (layer, pt[bi, p], 0, 0))

    args = [qb, row3(q), row3(kn), row3(vn)]
    in_specs = [pl.BlockSpec((None, N_HEADS, HEAD_DIM, LANE), lambda bi, pt: (bi, 0, 0, 0)),
                row_spec, row_spec, row_spec]
    if cache_lf is not None:
        args.append(jnp.broadcast_to(lfn[:, :, None], (nb, N_HEADS, LANE)))
        in_specs.append(pl.BlockSpec((None, N_HEADS, LANE), lambda bi, pt: (bi, 0, 0)))
    args += [eh, tri]
    in_specs += [fixed((N_HEADS, d)), fixed((page, page))]
    args += [kt] * n_pages + [vt] * n_pages
    in_specs += [page_spec(p) for p in range(n_pages)] * 2
    if cache_lf is not None:
        args += [jnp.transpose(cache_lf, (0, 1, 3, 2))] * n_pages
        in_specs += [lf_spec(p) for p in range(n_pages)]
        kern = functools.partial(_dec_c_kernel, n_pages=n_pages, page=page)
    else:
        assert n_pages % (MOBA_BLOCK // page) == 0, "past length must be whole MoBA blocks"
        kern = functools.partial(_dec_ab_kernel, n_pages=n_pages, page=page)
    grid_spec = pltpu.PrefetchScalarGridSpec(
        num_scalar_prefetch=1, grid=(nb,), in_specs=in_specs,
        out_specs=pl.BlockSpec((None, 1, d), lambda bi, pt: (bi, 0, 0)),
        scratch_shapes=[pltpu.VMEM((N_HEADS, n_pages * page), F32),
                        pltpu.VMEM((N_HEADS, n_pages * page), F32),
                        pltpu.VMEM((N_HEADS, HEAD_DIM, LANE), F32)])
    out = pl.pallas_call(
        kern, grid_spec=grid_spec, out_shape=jax.ShapeDtypeStruct((nb, 1, d), BF16),
        compiler_params=_params(1, DEC_VMEM),
        name="dec_c" if cache_lf is not None else "dec_ab")(page_table, *args)
    return out.reshape(nb, d)


def _rope_tables(pos):
    half = ROT_DIM // 2
    inv = ROPE_THETA ** (-jnp.arange(half, dtype=F32) * 2.0 / ROT_DIM)
    ang = pos.astype(F32)[:, None] * inv[None, :]
    cos, sin = jnp.cos(ang), jnp.sin(ang)
    rest = jnp.zeros((pos.shape[0], HEAD_DIM - ROT_DIM), F32)
    zero = jnp.zeros_like(sin)
    c = jnp.concatenate([cos, cos, rest + 1.0], axis=1)
    s1 = jnp.concatenate([-sin, zero, rest], axis=1)
    s2 = jnp.concatenate([zero, sin, rest], axis=1)
    return tuple(jnp.concatenate([a, a], axis=1) for a in (c, s1, s2))


def kernel(x_prompt, x_sample, cache_k_ab, cache_v_ab, cache_k_c, cache_v_c, cache_logf_c, state_conv,
           page_table, w_in_ab, w_o_ab, w_in_c, b_f_c, w_o_c, g_mix, g_ffn, w_up, conv_w, conv_b,
           w_down, g_final):
    b, t, d = x_prompt.shape
    nb = x_sample.shape[0]
    assert x_sample.shape[1] == 1 and d == N_HEADS * HEAD_DIM
    depth = g_mix.shape[0]
    past_len = page_table.shape[1] * cache_k_ab.shape[2]
    m = b * t
    tm = min(512, t)
    tq = min(512, t)
    tf = min(1024, t)
    cw = 256
    half = d // 2

    xp = x_prompt.reshape(m, d)
    xs = x_sample.reshape(nb, d)
    tabs_p = _rope_tables(jnp.arange(t, dtype=jnp.int32))
    tabs_s = _rope_tables(jnp.full((nb,), past_len, jnp.int32))

    outs = {k: [] for k in ("k_ab_p", "v_ab_p", "k_ab_s", "v_ab_s", "k_c_p", "v_c_p", "f_c_p",
                            "k_c_s", "v_c_s", "f_c_s", "conv_p", "conv_s")}
    for l in range(depth):
        i = l // 2
        gm = g_mix[l].reshape(1, d)
        if l % 2 == 0:
            w_in, w_o = w_in_ab[i].astype(BF16), w_o_ab[i].astype(BF16)
            wq, wk, wv = w_in[:, :d], w_in[:, d:2 * d], w_in[:, 2 * d:3 * d]
            (qb,) = _proj(xp, gm, wq, tm=tm, rope_tabs=tabs_p, emit_f32=False, out_scale=Q_SCALE)
            kf, kb = _proj(xp, gm, wk, tm=tm, rope_tabs=tabs_p)
            vf, vb = _proj(xp, gm, wv, tm=tm)
            n_hp = H_SB * HEAD_DIM // LANE
            o_sb = _sb_attention(qb, kb, vb, b=b, t=t, tq=tq, n_hp=n_hp)
            km = _kmean(kf, col_block=1, width=half).reshape(b, t // MOBA_BLOCK, half)
            km = jnp.pad(km, ((0, 0), (0, LANE - t // MOBA_BLOCK), (0, 0)))
            o_mb = _moba_attention(qb, kb, vb, km, b=b, t=t, tq=tq, n_hp=n_hp, hp0=n_hp)
            xp = _oproj(xp, [o_sb, o_mb], [w_o[:half], w_o[half:]], tm=tm)
            outs["k_ab_p"].append(kf.reshape(b, t, N_HEADS, HEAD_DIM))
            outs["v_ab_p"].append(vf.reshape(b, t, N_HEADS, HEAD_DIM))
            (qs,) = _proj(xs, gm, wq, tm=nb, rope_tabs=tabs_s, emit_f32=False, out_scale=Q_SCALE)
            ksf, _ = _proj(xs, gm, wk, tm=nb, rope_tabs=tabs_s)
            vsf, _ = _proj(xs, gm, wv, tm=nb)
            o_s = _decode_attention(i, page_table, qs.astype(F32), ksf, vsf, cache_k_ab, cache_v_ab)
            xs = _oproj(xs, [o_s], [w_o], tm=nb)
            outs["k_ab_s"].append(ksf.reshape(nb, 1, N_HEADS, HEAD_DIM))
            outs["v_ab_s"].append(vsf.reshape(nb, 1, N_HEADS, HEAD_DIM))
        else:
            w_in, w_o = w_in_c[i].astype(BF16), w_o_c[i].astype(BF16)
            wq, wk, wv = w_in[:, :d], w_in[:, d:2 * d], w_in[:, 2 * d:3 * d]
            wf = jnp.pad(w_in[:, 3 * d:], ((0, 0), (0, LANE - N_HEADS)))
            bf = jnp.pad(b_f_c[i], (0, LANE - N_HEADS)).reshape(1, LANE)
            qb, lf = _proj(xp, gm, wq, tm=tm, gate=(wf, bf), emit_f32=False, out_scale=Q_SCALE)
            kf, kb = _proj(xp, gm, wk, tm=tm)
            vf, vb = _proj(xp, gm, wv, tm=tm)
            logf = lf[:, :N_HEADS].reshape(b, t, N_HEADS)
            pieces = _cumsum_time(jnp.swapaxes(logf, 1, 2), tc=tq)
            ka = jnp.stack(pieces, axis=-1).reshape(b, N_HEADS // 2, 2, t, 3)
            ka = ka.transpose(0, 1, 3, 2, 4).reshape(b, N_HEADS // 2, t, 6)
            ka = jnp.pad(ka, ((0, 0), (0, 0), (0, 0), (0, LANE - 6)))
            o_c = _fox_attention(qb, kb, vb, ka, b=b, t=t, tq=tq)
            xp = _oproj(xp, [o_c], [w_o], tm=tm)
            outs["k_c_p"].append(kf.reshape(b, t, N_HEADS, HEAD_DIM))
            outs["v_c_p"].append(vf.reshape(b, t, N_HEADS, HEAD_DIM))
            outs["f_c_p"].append(logf)
            qs, lfs = _proj(xs, gm, wq, tm=nb, gate=(wf, bf), emit_f32=False, out_scale=Q_SCALE)
            ksf, _ = _proj(xs, gm, wk, tm=nb)
            vsf, _ = _proj(xs, gm, wv, tm=nb)
            o_s = _decode_attention(i, page_table, qs.astype(F32), ksf, vsf, cache_k_c, cache_v_c,
                                    cache_logf_c, lfs[:, :N_HEADS])
            xs = _oproj(xs, [o_s], [w_o], tm=nb)
            outs["k_c_s"].append(ksf.reshape(nb, 1, N_HEADS, HEAD_DIM))
            outs["v_c_s"].append(vsf.reshape(nb, 1, N_HEADS, HEAD_DIM))
            outs["f_c_s"].append(lfs[:, :N_HEADS].reshape(nb, 1, N_HEADS))

        gf = g_ffn[l].reshape(1, d)
        wu, wd = w_up[l].astype(BF16), w_down[l].astype(BF16)
        cb = conv_b[l].reshape(1, -1)
        yp, conv_p = _ffn_prompt(xp.reshape(b, t, d), gf, wu, conv_w[l], cb, wd, tm=tf, cw=cw)
        xp = yp.reshape(m, d)
        xs, conv_s = _ffn_sample(xs, gf, wu, conv_w[l], cb, wd, state_conv[l], cw=cw)
        outs["conv_p"].append(conv_p)
        outs["conv_s"].append(conv_s)

    gfin = g_final.reshape(1, d)
    y_prompt = _final_norm(xp, gfin, tm=tm).reshape(b, t, d)
    y_sample = _final_norm(xs, gfin, tm=nb).reshape(nb, 1, d)
    st = lambda key: jnp.stack(outs[key])
    return (y_prompt, y_sample, st("k_ab_p"), st("v_ab_p"), st("k_ab_s"), st("v_ab_s"),
            st("k_c_p"), st("v_c_p"), st("f_c_p"), st("k_c_s"), st("v_c_s"), st("f_c_s"),
            st("conv_p"), st("conv_s"))
```

```python
import functools

import numpy as np
import jax
import jax.numpy as jnp
from jax import lax
from jax.experimental import pallas as pl
from jax.experimental.pallas import tpu as pltpu

F32 = jnp.float32
BF16 = jnp.bfloat16

HEAD_DIM = 64
N_HEADS = 16
H_SB = N_HEADS // 2
ROT_DIM = HEAD_DIM // 4
ROPE_THETA = 500000.0
MOBA_BLOCK = 256
MOBA_TOPK = 3
RMS_EPS = 1e-6
ATTN_SCALE = HEAD_DIM ** -0.5
CONV_WIDTH = 3
LOG2E = 1.4426950408889634
Q_SCALE = ATTN_SCALE * LOG2E

LANE = 128
SUBLANE = 8
NEG = -1e30
EXIT_LOG2 = -160.0
PROJ_VMEM = 48 * 1024 * 1024
DEC_VMEM = 56 * 1024 * 1024

_NT = (((1,), (1,)), ((), ()))


def _params(n_axes, vmem=PROJ_VMEM):
    return pltpu.CompilerParams(dimension_semantics=("arbitrary",) * n_axes,
                                vmem_limit_bytes=vmem)


def _rms(x, g):
    ms = jnp.mean(x * x, axis=-1, keepdims=True)
    return x * lax.rsqrt(ms + RMS_EPS) * g


def _log_sigmoid(z):
    return jnp.minimum(z, 0.0) - jnp.log1p(jnp.exp(-jnp.abs(z)))


def _log2_sigmoid(z2):
    return jnp.minimum(z2, 0.0) - jnp.log2(1.0 + jnp.exp2(-jnp.abs(z2)))


def _split2(x):
    hi = x.astype(BF16)
    lo = (x - hi.astype(F32)).astype(BF16)
    return hi, lo


def _proj_kernel(*refs, rope, emit_f32, gate, out_scale):
    it = iter(refs)
    x_ref, g_ref, w_ref = next(it), next(it), next(it)
    if rope:
        c_ref, s1_ref, s2_ref = next(it), next(it), next(it)
    if gate:
        wf_ref, bf_ref = next(it), next(it)
    if emit_f32:
        of_ref = next(it)
    ob_ref = next(it)
    if gate:
        og_ref = next(it)

    xn = _rms(x_ref[...], g_ref[...]).astype(BF16)
    y = jnp.dot(xn, w_ref[...], preferred_element_type=F32)
    if rope:
        half = y.shape[1] // 2
        c, s1, s2 = c_ref[...], s1_ref[...], s2_ref[...]
        parts = [y[:, :half]]
        for ch in range(half // LANE):
            yc = y[:, half + ch * LANE: half + (ch + 1) * LANE]
            up = pltpu.roll(yc, LANE - ROT_DIM // 2, 1)
            dn = pltpu.roll(yc, ROT_DIM // 2, 1)
            parts.append(yc * c + up * s1 + dn * s2)
        y = jnp.concatenate(parts, axis=1)
    if emit_f32:
        of_ref[...] = y
    ob_ref[...] = (y * out_scale if out_scale != 1.0 else y).astype(BF16)
    if gate:
        f = jnp.dot(xn, wf_ref[...], preferred_element_type=F32) + bf_ref[...]
        og_ref[...] = _log_sigmoid(f)


def _proj(x, g, w, *, tm, rope_tabs=None, gate=None, emit_f32=True, out_scale=1.0):
    m, d = x.shape
    n = w.shape[1]
    row = lambda i: (i, 0)
    fixed = lambda i: (0, 0)
    in_specs = [pl.BlockSpec((tm, d), row), pl.BlockSpec((1, d), fixed), pl.BlockSpec((d, n), fixed)]
    args = [x, g, w]
    if rope_tabs is not None:
        n_tab_blocks = rope_tabs[0].shape[0] // tm
        tab = lambda i: (i % n_tab_blocks, 0)
        in_specs += [pl.BlockSpec((tm, LANE), tab)] * 3
        args += list(rope_tabs)
    if gate is not None:
        in_specs += [pl.BlockSpec((d, LANE), fixed), pl.BlockSpec((1, LANE), fixed)]
        args += list(gate)
    out_shape, out_specs = [], []
    if emit_f32:
        out_shape.append(jax.ShapeDtypeStruct((m, n), F32))
        out_specs.append(pl.BlockSpec((tm, n), row))
    out_shape.append(jax.ShapeDtypeStruct((m, n), BF16))
    out_specs.append(pl.BlockSpec((tm, n), row))
    if gate is not None:
        out_shape.append(jax.ShapeDtypeStruct((m, LANE), F32))
        out_specs.append(pl.BlockSpec((tm, LANE), row))
    kern = functools.partial(_proj_kernel, rope=rope_tabs is not None, emit_f32=emit_f32,
                             gate=gate is not None, out_scale=out_scale)
    return pl.pallas_call(kern, grid=(m // tm,), in_specs=in_specs, out_specs=out_specs,
                          out_shape=out_shape, compiler_params=_params(1), name="proj")(*args)


def _oproj_kernel(x_ref, *refs, n_parts):
    o_refs, w_refs, out_ref = refs[:n_parts], refs[n_parts:2 * n_parts], refs[2 * n_parts]
    y = jnp.dot(o_refs[0][...], w_refs[0][...], preferred_element_type=F32)
    for o_ref, w_ref in zip(o_refs[1:], w_refs[1:]):
        y = y + jnp.dot(o_ref[...], w_ref[...], preferred_element_type=F32)
    out_ref[...] = x_ref[...] + y


def _oproj(x, parts, weights, *, tm):
    m, d = x.shape
    row = lambda i: (i, 0)
    fixed = lambda i: (0, 0)
    in_specs = [pl.BlockSpec((tm, d), row)]
    in_specs += [pl.BlockSpec((tm, p.shape[1]), row) for p in parts]
    in_specs += [pl.BlockSpec(w.shape, fixed) for w in weights]
    return pl.pallas_call(
        functools.partial(_oproj_kernel, n_parts=len(parts)), grid=(m // tm,),
        in_specs=in_specs, out_specs=pl.BlockSpec((tm, d), row),
        out_shape=jax.ShapeDtypeStruct((m, d), F32), compiler_params=_params(1),
        name="oproj")(x, *parts, *weights)


def _silu_gate(gt, vl):
    return gt * (1.0 / (1.0 + jnp.exp(-gt))) * vl


def _ffn_prompt_kernel(x_ref, g_ref, wu_ref, cw_ref, cb_ref, wd_ref, out_ref, st_ref,
                       xn_sc, act_sc, ug_sc, uv_sc, carry_sc, *, cw):
    tm = x_ref.shape[0]
    dff = wd_ref.shape[0]

    @pl.when(pl.program_id(1) == 0)
    def _():
        carry_sc[...] = jnp.zeros(carry_sc.shape, F32)

    x = x_ref[...]
    xn_sc[...] = _rms(x, g_ref[...]).astype(BF16)

    def conv(col, u_sc):
        cols = slice(col, col + cw)
        u = jnp.dot(xn_sc[...], wu_ref[:, cols], preferred_element_type=F32)
        u_sc[0:SUBLANE, :] = carry_sc[:, cols]
        u_sc[SUBLANE:SUBLANE + tm, :] = u
        tail = u[tm - SUBLANE:tm, :]
        carry_sc[:, cols] = tail
        st_ref[:, cols] = tail
        w = cw_ref[:, cols]
        u0 = u_sc[SUBLANE - 2:SUBLANE - 2 + tm, :]
        u1 = u_sc[SUBLANE - 1:SUBLANE - 1 + tm, :]
        return ((cb_ref[:, cols] + w[0:1] * u0) + w[1:2] * u1) + w[2:3] * u

    for c in range(dff // cw):
        gt = conv(c * cw, ug_sc)
        vl = conv(dff + c * cw, uv_sc)
        act_sc[:, c * cw:(c + 1) * cw] = _silu_gate(gt, vl).astype(BF16)
    out_ref[...] = x + jnp.dot(act_sc[...], wd_ref[...], preferred_element_type=F32)


def _ffn_prompt(x, g, w_up, conv_w, conv_b, w_down, *, tm, cw):
    b, t, d = x.shape
    dff = w_down.shape[0]
    xmap = lambda bi, i: (bi, i, 0)
    resident = lambda shape: pl.BlockSpec(shape, lambda bi, i: (0, 0), pipeline_mode=pl.Buffered(1))
    in_specs = [pl.BlockSpec((None, tm, d), xmap), resident((1, d)), resident((d, 2 * dff)),
                resident((CONV_WIDTH, 2 * dff)), resident((1, 2 * dff)), resident((dff, d))]
    out_specs = [pl.BlockSpec((None, tm, d), xmap),
                 pl.BlockSpec((None, None, SUBLANE, 2 * dff), lambda bi, i: (bi, i, 0, 0))]
    out_shape = [jax.ShapeDtypeStruct((b, t, d), F32),
                 jax.ShapeDtypeStruct((b, t // tm, SUBLANE, 2 * dff), F32)]
    scratch = [pltpu.VMEM((tm, d), BF16), pltpu.VMEM((tm, dff), BF16),
               pltpu.VMEM((tm + SUBLANE, cw), F32), pltpu.VMEM((tm + SUBLANE, cw), F32),
               pltpu.VMEM((SUBLANE, 2 * dff), F32)]
    y, st = pl.pallas_call(
        functools.partial(_ffn_prompt_kernel, cw=cw), grid=(b, t // tm), in_specs=in_specs,
        out_specs=out_specs, out_shape=out_shape, scratch_shapes=scratch,
        compiler_params=_params(2, DEC_VMEM), name="ffn_prompt")(x, g, w_up, conv_w, conv_b, w_down)
    return y, st[:, -1, SUBLANE - (CONV_WIDTH - 1):, :]


def _ffn_sample_kernel(x_ref, g_ref, wg_ref, wv_ref, cwg_ref, cwv_ref, cbg_ref, cbv_ref, wd_ref,
                       s0g_ref, s0v_ref, s1g_ref, s1v_ref, out_ref, ug_ref, uv_ref, xn_sc, acc_sc):
    c = pl.program_id(0)

    @pl.when(c == 0)
    def _():
        x = x_ref[...]
        xn_sc[...] = _rms(x, g_ref[...]).astype(BF16)
        acc_sc[...] = x

    xn = xn_sc[...]

    def conv(w_ref, cw_ref, cb_ref, s0_ref, s1_ref, u_ref):
        u = jnp.dot(xn, w_ref[...], preferred_element_type=F32)
        u_ref[...] = u
        cw = cw_ref[...]
        return ((cb_ref[...] + cw[0:1] * s0_ref[...]) + cw[1:2] * s1_ref[...]) + cw[2:3] * u

    gt = conv(wg_ref, cwg_ref, cbg_ref, s0g_ref, s1g_ref, ug_ref)
    vl = conv(wv_ref, cwv_ref, cbv_ref, s0v_ref, s1v_ref, uv_ref)
    act = _silu_gate(gt, vl).astype(BF16)
    acc_sc[...] += jnp.dot(act, wd_ref[...], preferred_element_type=F32)

    @pl.when(c == pl.num_programs(0) - 1)
    def _():
        out_ref[...] = acc_sc[...]


def _ffn_sample(x, g, w_up, conv_w, conv_b, w_down, state, *, cw):
    m, d = x.shape
    dff = w_down.shape[0]
    nc = dff // cw
    st = state.reshape(m, (CONV_WIDTH - 1) * 2 * dff)
    fixed = lambda c: (0, 0)
    in_specs = [
        pl.BlockSpec((m, d), fixed),
        pl.BlockSpec((1, d), fixed),
        pl.BlockSpec((d, cw), lambda c: (0, c)),
        pl.BlockSpec((d, cw), lambda c: (0, c + nc)),
        pl.BlockSpec((CONV_WIDTH, cw), lambda c: (0, c)),
        pl.BlockSpec((CONV_WIDTH, cw), lambda c: (0, c + nc)),
        pl.BlockSpec((1, cw), lambda c: (0, c)),
        pl.BlockSpec((1, cw), lambda c: (0, c + nc)),
        pl.BlockSpec((cw, d), lambda c: (c, 0)),
        pl.BlockSpec((m, cw), lambda c: (0, c)),
        pl.BlockSpec((m, cw), lambda c: (0, c + nc)),
        pl.BlockSpec((m, cw), lambda c: (0, c + 2 * nc)),
        pl.BlockSpec((m, cw), lambda c: (0, c + 3 * nc)),
    ]
    out_specs = [pl.BlockSpec((m, d), fixed),
                 pl.BlockSpec((m, cw), lambda c: (0, c)),
                 pl.BlockSpec((m, cw), lambda c: (0, c))]
    out_shape = [jax.ShapeDtypeStruct((m, d), F32),
                 jax.ShapeDtypeStruct((m, dff), F32),
                 jax.ShapeDtypeStruct((m, dff), F32)]
    scratch = [pltpu.VMEM((m, d), BF16), pltpu.VMEM((m, d), F32)]
    y, ug, uv = pl.pallas_call(
        _ffn_sample_kernel, grid=(nc,), in_specs=in_specs, out_specs=out_specs,
        out_shape=out_shape, scratch_shapes=scratch, compiler_params=_params(1),
        name="ffn_sample")(x, g, w_up, w_up, conv_w, conv_w, conv_b, conv_b, w_down, st, st, st, st)
    u = jnp.concatenate([ug, uv], axis=-1)
    new_state = jnp.stack([state[:, 1, :], u], axis=1)
    return y, new_state


def _norm_kernel(x_ref, g_ref, o_ref):
    o_ref[...] = _rms(x_ref[...], g_ref[...])


def _final_norm(x, g, *, tm):
    m, d = x.shape
    return pl.pallas_call(
        _norm_kernel, grid=(m // tm,),
        in_specs=[pl.BlockSpec((tm, d), lambda i: (i, 0)), pl.BlockSpec((1, d), lambda i: (0, 0))],
        out_specs=pl.BlockSpec((tm, d), lambda i: (i, 0)),
        out_shape=jax.ShapeDtypeStruct((m, d), F32), compiler_params=_params(1),
        name="final_norm")(x, g)


def _split3(x):
    hi = x.astype(BF16)
    r1 = x - hi.astype(F32)
    mid = r1.astype(BF16)
    lo = (r1 - mid.astype(F32)).astype(BF16)
    return hi, mid, lo


def _cumsum_kernel(x_ref, u_ref, hi_ref, mid_ref, lo_ref, carry_sc):
    @pl.when(pl.program_id(1) == 0)
    def _():
        carry_sc[...] = jnp.zeros_like(carry_sc)

    u = u_ref[...]
    hi, mid, lo = _split3(x_ref[...])
    c = (jnp.dot(hi, u, preferred_element_type=F32) + jnp.dot(mid, u, preferred_element_type=F32)
         + jnp.dot(lo, u, preferred_element_type=F32)) + carry_sc[...]
    carry_sc[...] = c[:, c.shape[1] - 1:]
    hi_ref[...], mid_ref[...], lo_ref[...] = _split3(c * (-LOG2E))


def _cumsum_time(logf_t, *, tc):
    b, h, t = logf_t.shape
    upper = jnp.asarray(np.triu(np.ones((tc, tc), np.float32)), BF16)
    spec = pl.BlockSpec((None, h, tc), lambda bi, i: (bi, 0, i))
    return pl.pallas_call(
        _cumsum_kernel, grid=(b, t // tc),
        in_specs=[spec, pl.BlockSpec((tc, tc), lambda bi, i: (0, 0))],
        out_specs=[spec] * 3,
        out_shape=[jax.ShapeDtypeStruct((b, h, t), BF16)] * 3,
        scratch_shapes=[pltpu.VMEM((h, 1), F32)], compiler_params=_params(2),
        name="cumsum")(logf_t, upper)


def _kmean_kernel(k_ref, o_ref):
    o_ref[...] = jnp.mean(k_ref[...], axis=0, keepdims=True)


def _kmean(k_flat, *, col_block, width):
    m = k_flat.shape[0]
    nblk = m // MOBA_BLOCK
    out = pl.pallas_call(
        _kmean_kernel, grid=(nblk,),
        in_specs=[pl.BlockSpec((MOBA_BLOCK, width), lambda n: (n, col_block))],
        out_specs=pl.BlockSpec((None, 1, width), lambda n: (n, 0, 0)),
        out_shape=jax.ShapeDtypeStruct((nblk, 1, width), F32), compiler_params=_params(1),
        name="kmean")(k_flat)
    return out.reshape(nblk, width)


def _stack_heads(q2, tq):
    lane = lax.broadcasted_iota(jnp.int32, (tq, LANE), 1)
    zero = jnp.zeros_like(q2)
    return jnp.where(lane < HEAD_DIM, q2, zero), jnp.where(lane >= HEAD_DIM, q2, zero)


def _unstack_heads(o, tq):
    lane = lax.broadcasted_iota(jnp.int32, (tq, LANE), 1)
    return jnp.where(lane < HEAD_DIM, o[:tq], o[tq:])


def _causal_keep(tq, tk, strict):
    r = lax.broadcasted_iota(jnp.int32, (2 * tq, tk), 0)
    r = jnp.where(r >= tq, r - tq, r)
    c = lax.broadcasted_iota(jnp.int32, (2 * tq, tk), 1)
    return (c < r) if strict else (c <= r)


def _softmax_step(s, v2, m_sc, acc_sc):
    tq = s.shape[0] // 2
    m_prev = m_sc[...]
    m_new = jnp.maximum(m_prev, jnp.max(s, axis=1, keepdims=True))
    alpha = jnp.exp2(m_prev - m_new)
    p = jnp.exp2(s - jnp.concatenate([m_new] * (s.shape[1] // LANE), axis=1)).astype(BF16)
    lane = lax.broadcasted_iota(jnp.int32, v2.shape, 1)
    one = jnp.ones_like(v2)
    pv = jnp.concatenate(
        [jnp.dot(p[:tq], jnp.where(lane < HEAD_DIM, v2, one), preferred_element_type=F32),
         jnp.dot(p[tq:], jnp.where(lane >= HEAD_DIM, v2, one), preferred_element_type=F32)], axis=0)
    acc_sc[...] = alpha * acc_sc[...] + pv
    m_sc[...] = m_new


def _softmax_finish(acc_sc, tq):
    acc = acc_sc[...]
    o = acc / pltpu.roll(acc, HEAD_DIM, 1)
    return _unstack_heads(o, tq).astype(BF16)


def _flash_loop(qi, tq, lhs_sc, k_ref, aug_ref, v_ref, m_sc, acc_sc, negligible=None):
    m_sc[...] = jnp.full(m_sc.shape, NEG, F32)
    acc_sc[...] = jnp.zeros(acc_sc.shape, F32)

    def chunk(j, diag):
        off = pl.multiple_of(j * tq, tq)
        kaug = jnp.concatenate([k_ref[pl.ds(off, tq), :], aug_ref[pl.ds(off, tq), :]], axis=1)
        s = lax.dot_general(lhs_sc[...], kaug, _NT, preferred_element_type=F32)
        if diag:
            s = jnp.where(_causal_keep(tq, tq, False), s, NEG)
        _softmax_step(s, v_ref[pl.ds(off, tq), :], m_sc, acc_sc)

    chunk(qi, True)

    def cond(jj):
        more = jj < qi
        if negligible is None:
            return more
        return jnp.logical_and(more, jnp.logical_not(negligible(jnp.maximum(qi - 1 - jj, 0))))

    def body(jj):
        chunk(qi - 1 - jj, False)
        return jj + 1

    lax.while_loop(cond, body, jnp.int32(0))


def _fox_kernel(q_ref, k_ref, v_ref, ka_ref, o_ref, lhs_sc, m_sc, acc_sc, ub_sc, kmax_sc, bmax_sc,
                *, tq):
    qi = pl.program_id(2)
    n_pieces = 3
    qa, qb = _stack_heads(q_ref[...], tq)
    lhs_sc[0:tq, 0:LANE] = qa
    lhs_sc[tq:2 * tq, 0:LANE] = qb
    lane = lax.broadcasted_iota(jnp.int32, (2 * tq, LANE), 1)
    row = lax.broadcasted_iota(jnp.int32, (2 * tq, LANE), 0)
    piece_head = jnp.where(lane < n_pieces, 0, jnp.where(lane < 2 * n_pieces, 1, 2))
    lhs_sc[:, LANE:2 * LANE] = jnp.where(piece_head == jnp.where(row >= tq, 1, 0), 1.0, 0.0).astype(BF16)

    @pl.when(qi == 0)
    def _():
        def kbody(j, acc):
            blk = k_ref[pl.ds(pl.multiple_of(j * tq, tq), tq), :].astype(F32)
            return jnp.maximum(acc, jnp.max(jnp.abs(blk), axis=0, keepdims=True))

        n_chunks = k_ref.shape[0] // tq
        kabs = lax.fori_loop(0, n_chunks, kbody, jnp.zeros((1, LANE), F32))
        lane1 = lax.broadcasted_iota(jnp.int32, (1, LANE), 1)
        kmax_sc[0] = jnp.max(jnp.where(lane1 < HEAD_DIM, kabs, 0.0))
        kmax_sc[1] = jnp.max(jnp.where(lane1 >= HEAD_DIM, kabs, 0.0))

        def bbody(j, carry):
            off = pl.multiple_of(j * tq + tq - 2 * SUBLANE, 2 * SUBLANE)
            tail = ka_ref[pl.ds(off, 2 * SUBLANE), :].astype(F32)
            r16 = lax.broadcasted_iota(jnp.int32, tail.shape, 0)
            l16 = lax.broadcasted_iota(jnp.int32, tail.shape, 1)
            last = jnp.where(r16 == 2 * SUBLANE - 1, tail, 0.0)
            bmax_sc[2 * j] = jnp.sum(jnp.where(l16 < n_pieces, last, 0.0))
            bmax_sc[2 * j + 1] = jnp.sum(
                jnp.where(l16 < n_pieces, 0.0, jnp.where(l16 < 2 * n_pieces, last, 0.0)))
            return carry

        lax.fori_loop(0, n_chunks, bbody, 0)

    q1 = jnp.sum(jnp.abs(lhs_sc[:, 0:LANE].astype(F32)), axis=1, keepdims=True)
    ub_sc[...] = jnp.broadcast_to(q1, (2 * tq, LANE)) * jnp.where(row >= tq, kmax_sc[1], kmax_sc[0])

    def negligible(j):
        gap = ub_sc[...] + jnp.where(row >= tq, bmax_sc[2 * j + 1], bmax_sc[2 * j]) - m_sc[...]
        return jnp.max(gap) < EXIT_LOG2

    _flash_loop(qi, tq, lhs_sc, k_ref, ka_ref, v_ref, m_sc, acc_sc, negligible)
    o_ref[...] = _softmax_finish(acc_sc, tq)


def _flash_scratch(tq):
    return [pltpu.VMEM((2 * tq, 2 * LANE), BF16), pltpu.VMEM((2 * tq, LANE), F32),
            pltpu.VMEM((2 * tq, LANE), F32)]


def _fox_attention(q, k, v, ka, *, b, t, tq):
    m = q.shape[0]
    n_hp = q.shape[1] // LANE
    nq = t // tq
    return pl.pallas_call(
        functools.partial(_fox_kernel, tq=tq), grid=(b, n_hp, nq),
        in_specs=[pl.BlockSpec((tq, LANE), lambda bi, hp, qi: (bi * nq + qi, hp)),
                  pl.BlockSpec((t, LANE), lambda bi, hp, qi: (bi, hp)),
                  pl.BlockSpec((t, LANE), lambda bi, hp, qi: (bi, hp)),
                  pl.BlockSpec((None, None, t, LANE), lambda bi, hp, qi: (bi, hp, 0, 0))],
        out_specs=pl.BlockSpec((tq, LANE), lambda bi, hp, qi: (bi * nq + qi, hp)),
        out_shape=jax.ShapeDtypeStruct((m, n_hp * LANE), BF16),
        scratch_shapes=_flash_scratch(tq) + [pltpu.VMEM((2 * tq, LANE), F32), pltpu.SMEM((2,), F32),
                                             pltpu.SMEM((2 * nq,), F32)],
        compiler_params=_params(3), name="fox_attn")(q, k, v, ka)


def _sb_kernel(q_ref, k_ref, v_ref, u_ref, o_ref, qs_sc, r_sc, acc_sc, *, tq, sub):
    qi = pl.program_id(2)
    qa, qb = _stack_heads(q_ref[...], tq)
    qs_sc[0:tq, :] = qa
    qs_sc[tq:2 * tq, :] = qb
    r_sc[...] = jnp.zeros(r_sc.shape, F32)
    acc_sc[...] = jnp.zeros(acc_sc.shape, F32)
    u = u_ref[...]

    def chunk(j, diag):
        off = pl.multiple_of(j * tq, tq)
        v2 = v_ref[pl.ds(off, tq), :]
        z_all = lax.dot_general(qs_sc[...], k_ref[pl.ds(off, tq), :], _NT, preferred_element_type=F32)
        keep_all = _causal_keep(tq, tq, True) if diag else None
        for sb in reversed(range(tq // sub)):
            z = z_all[:, sb * sub:(sb + 1) * sub]
            neg_abs = lax.bitcast_convert_type(
                lax.bitcast_convert_type(z, jnp.uint32) | jnp.uint32(0x80000000), F32)
            sp = jnp.maximum(z, 0.0) + jnp.log2(1.0 + jnp.exp2(neg_abs))
            if diag:
                keep = keep_all[:, sb * sub:(sb + 1) * sub]
                sp = jnp.where(keep, sp, 0.0)
            cum = jnp.dot(sp.astype(BF16), u, preferred_element_type=F32)
            r = r_sc[...]
            w = jnp.exp2((z - sp) + (cum + jnp.concatenate([r] * (sub // LANE), axis=1)))
            if diag:
                w = jnp.where(keep, w, 0.0)
            acc_sc[...] += jnp.dot(w.astype(BF16), v2[sb * sub:(sb + 1) * sub, :],
                                   preferred_element_type=F32)
            r_sc[...] = r + (cum[:, 0:1] - sp[:, 0:1])

    chunk(qi, True)

    def cond(jj):
        return jnp.logical_and(jj < qi, jnp.max(r_sc[...]) >= EXIT_LOG2)

    def body(jj):
        chunk(qi - 1 - jj, False)
        return jj + 1

    lax.while_loop(cond, body, jnp.int32(0))
    o_ref[...] = _unstack_heads(acc_sc[...], tq).astype(BF16)


def _sb_attention(q, k, v, *, b, t, tq, n_hp, sub=256):
    m = q.shape[0]
    nq = t // tq
    lower = jnp.asarray(-np.tril(np.ones((sub, sub), np.float32), -1), BF16)
    scratch = [pltpu.VMEM((2 * tq, LANE), BF16), pltpu.VMEM((2 * tq, LANE), F32),
               pltpu.VMEM((2 * tq, LANE), F32)]
    return pl.pallas_call(
        functools.partial(_sb_kernel, tq=tq, sub=sub), grid=(b, n_hp, nq),
        in_specs=[pl.BlockSpec((tq, LANE), lambda bi, hp, qi: (bi * nq + qi, hp)),
                  pl.BlockSpec((t, LANE), lambda bi, hp, qi: (bi, hp)),
                  pl.BlockSpec((t, LANE), lambda bi, hp, qi: (bi, hp)),
                  pl.BlockSpec((sub, sub), lambda bi, hp, qi: (0, 0))],
        out_specs=pl.BlockSpec((tq, LANE), lambda bi, hp, qi: (bi * nq + qi, hp)),
        out_shape=jax.ShapeDtypeStruct((m, n_hp * LANE), BF16),
        scratch_shapes=scratch, compiler_params=_params(3), name="sb_attn")(q, k, v, lower)


def _moba_kernel(q_ref, k_ref, v_ref, km_ref, et_ref, o_ref, lhs_sc, m_sc, acc_sc, *, tq):
    qi = pl.program_id(2)
    qa, qb = _stack_heads(q_ref[...], tq)
    lhs_sc[0:tq, 0:LANE] = qa
    lhs_sc[tq:2 * tq, 0:LANE] = qb

    qs = lhs_sc[:, 0:LANE]
    km_hi, km_lo = _split2(km_ref[...])
    gate = (lax.dot_general(qs, km_hi, _NT, preferred_element_type=F32)
            + lax.dot_general(qs, km_lo, _NT, preferred_element_type=F32))
    n_idx = lax.broadcasted_iota(jnp.int32, (2 * tq, LANE), 1)
    row = lax.broadcasted_iota(jnp.int32, (2 * tq, LANE), 0)
    pos = qi * tq + jnp.where(row >= tq, row - tq, row)
    q_blk = jnp.right_shift(pos, MOBA_BLOCK.bit_length() - 1)
    n_f = n_idx.astype(F32)
    g = jnp.where(n_idx < q_blk, gate, -jnp.inf)
    bias = jnp.where(n_idx == q_blk, 0.0, NEG)
    for _ in range(MOBA_TOPK):
        mx = jnp.max(g, axis=1, keepdims=True)
        first = jnp.min(jnp.where(g == mx, n_f, float(LANE)), axis=1, keepdims=True)
        hit = n_f == first
        bias = jnp.where(jnp.where(hit, n_idx, q_blk) < q_blk, 0.0, bias)
        g = jnp.where(hit, -jnp.inf, g)
    lhs_sc[:, LANE:2 * LANE] = bias.astype(BF16)
    _flash_loop(qi, tq, lhs_sc, k_ref, et_ref, v_ref, m_sc, acc_sc)
    o_ref[...] = _softmax_finish(acc_sc, tq)


def _moba_attention(q, k, v, kmean_pad, *, b, t, tq, n_hp, hp0):
    m = q.shape[0]
    nq = t // tq
    et = (np.arange(t)[:, None] // MOBA_BLOCK == np.arange(LANE)[None, :]).astype(np.float32)
    et = jnp.asarray(et, BF16)
    scratch = _flash_scratch(tq)
    return pl.pallas_call(
        functools.partial(_moba_kernel, tq=tq), grid=(b, n_hp, nq),
        in_specs=[pl.BlockSpec((tq, LANE), lambda bi, hp, qi: (bi * nq + qi, hp0 + hp)),
                  pl.BlockSpec((t, LANE), lambda bi, hp, qi: (bi, hp0 + hp)),
                  pl.BlockSpec((t, LANE), lambda bi, hp, qi: (bi, hp0 + hp)),
                  pl.BlockSpec((None, LANE, LANE), lambda bi, hp, qi: (bi, 0, hp)),
                  pl.BlockSpec((t, LANE), lambda bi, hp, qi: (0, 0))],
        out_specs=pl.BlockSpec((tq, LANE), lambda bi, hp, qi: (bi * nq + qi, hp)),
        out_shape=jax.ShapeDtypeStruct((m, n_hp * LANE), BF16),
        scratch_shapes=scratch, compiler_params=_params(3), name="moba_attn")(q, k, v, kmean_pad, et)


def _dec_scores(qb_ref, kt_refs, s_sc, page):
    for h in range(N_HEADS):
        qh = qb_ref[h]
        for p, kt_ref in enumerate(kt_refs):
            s_sc[h:h + 1, p * page:(p + 1) * page] = jnp.sum(kt_ref[h] * qh, axis=0, keepdims=True)


def _dec_new_score(qrow_ref, kn_ref, eh):
    prod = jnp.broadcast_to(kn_ref[...] * qrow_ref[...], (LANE, eh.shape[1]))
    hi, lo = _split2(prod)
    return (lax.dot_general(eh, hi, _NT, preferred_element_type=F32)
            + lax.dot_general(eh, lo, _NT, preferred_element_type=F32))


def _lane_suffix(x, tri):
    hi, lo = _split2(x)
    return jnp.dot(hi, tri, preferred_element_type=F32) + jnp.dot(lo, tri, preferred_element_type=F32)


def _dec_values(w_sc, vt_refs, acc_sc, page):
    for h in range(N_HEADS):
        acc = jnp.zeros(acc_sc.shape[1:], F32)
        for p, vt_ref in enumerate(vt_refs):
            acc = acc + vt_ref[h] * w_sc[h:h + 1, p * page:(p + 1) * page]
        acc_sc[h] = acc


def _dec_output(acc_sc, p_new, vn_ref, eh, o_ref):
    d = eh.shape[1]
    hi, lo = _split2(acc_sc[...].reshape(d, LANE))
    ones = jnp.ones((SUBLANE, LANE), BF16)
    o = (lax.dot_general(ones, hi, _NT, preferred_element_type=F32)
         + lax.dot_general(ones, lo, _NT, preferred_element_type=F32))[0:1]
    pn = jnp.sum(jnp.concatenate([p_new] * (d // LANE), axis=1) * eh.astype(F32), axis=0, keepdims=True)
    o_ref[...] = (o + pn * vn_ref[...]).astype(BF16)


def _dec_ab_kernel(pt_ref, qb_ref, qrow_ref, kn_ref, vn_ref, eh_ref, tri_ref, *rest, n_pages, page):
    kt_refs, vt_refs = rest[:n_pages], rest[n_pages:2 * n_pages]
    o_ref, s_sc, w_sc, acc_sc = rest[2 * n_pages:2 * n_pages + 4]
    eh, tri = eh_ref[...], tri_ref[...]
    lane_f = lax.broadcasted_iota(jnp.int32, (N_HEADS, LANE), 1).astype(F32)
    is_sb = lax.broadcasted_iota(jnp.int32, (N_HEADS, LANE), 0) < H_SB

    _dec_scores(qb_ref, kt_refs, s_sc, page)
    s_new = _dec_new_score(qrow_ref, kn_ref, eh)

    ppb = MOBA_BLOCK // page
    nblk = n_pages // ppb
    g = jnp.full((N_HEADS, LANE), -jnp.inf, F32)
    for n in range(nblk):
        gn = jnp.sum(s_sc[:, n * MOBA_BLOCK:(n + 1) * MOBA_BLOCK], axis=1, keepdims=True)
        g = jnp.where(lane_f == float(n), gn, g)
    firsts = []
    for _ in range(min(MOBA_TOPK, nblk + 1)):
        mx = jnp.max(g, axis=1, keepdims=True)
        first = jnp.min(jnp.where(g == mx, lane_f, float(LANE)), axis=1, keepdims=True)
        firsts.append(first)
        g = jnp.where(lane_f == first, -jnp.inf, g)
    sel_bias = []
    for n in range(nblk):
        miss = functools.reduce(jnp.minimum, [jnp.abs(f - float(n)) for f in firsts])
        sel_bias.append(jnp.where(miss == 0.0, 0.0, NEG))
    m = s_new
    for p in range(n_pages):
        sc = s_sc[:, p * page:(p + 1) * page] + sel_bias[p // ppb]
        m = jnp.maximum(m, jnp.max(sc, axis=1, keepdims=True))

    carry = jnp.zeros((N_HEADS, LANE), F32)
    lsum = jnp.zeros((N_HEADS, LANE), F32)
    for p in reversed(range(n_pages)):
        z = s_sc[:, p * page:(p + 1) * page]
        ls = _log2_sigmoid(z)
        l1m = ls - z
        cum = _lane_suffix(l1m, tri)
        w_sb = jnp.exp2(ls + (cum + carry))
        carry = carry + (cum[:, 0:1] + l1m[:, 0:1])
        pm = jnp.exp2((z + sel_bias[p // ppb]) - m)
        lsum = lsum + pm
        w_sc[:, p * page:(p + 1) * page] = jnp.where(is_sb, w_sb, pm)
    p_new = jnp.where(is_sb, 0.0, jnp.exp2(s_new - m))
    scale = jnp.where(is_sb, 1.0, 1.0 / (jnp.sum(lsum, axis=1, keepdims=True) + p_new))
    for p in range(n_pages):
        w_sc[:, p * page:(p + 1) * page] = w_sc[:, p * page:(p + 1) * page] * scale
    _dec_values(w_sc, vt_refs, acc_sc, page)
    _dec_output(acc_sc, p_new * scale, vn_ref, eh, o_ref)


def _dec_c_kernel(pt_ref, qb_ref, qrow_ref, kn_ref, vn_ref, lfn_ref, eh_ref, tri_ref, *rest,
                  n_pages, page):
    kt_refs, vt_refs = rest[:n_pages], rest[n_pages:2 * n_pages]
    lf_refs = rest[2 * n_pages:3 * n_pages]
    o_ref, s_sc, w_sc, acc_sc = rest[3 * n_pages:3 * n_pages + 4]
    eh, tri = eh_ref[...], tri_ref[...]

    _dec_scores(qb_ref, kt_refs, s_sc, page)
    s_new = _dec_new_score(qrow_ref, kn_ref, eh)

    carry = lfn_ref[...] * LOG2E
    m = s_new
    for p in reversed(range(n_pages)):
        lf = lf_refs[p][...] * LOG2E
        cum = _lane_suffix(lf, tri)
        sc = s_sc[:, p * page:(p + 1) * page] + (cum + carry)
        s_sc[:, p * page:(p + 1) * page] = sc
        carry = carry + (cum[:, 0:1] + lf[:, 0:1])
        m = jnp.maximum(m, jnp.max(sc, axis=1, keepdims=True))

    lsum = jnp.zeros((N_HEADS, LANE), F32)
    for p in range(n_pages):
        pm = jnp.exp2(s_sc[:, p * page:(p + 1) * page] - m)
        lsum = lsum + pm
        w_sc[:, p * page:(p + 1) * page] = pm
    p_new = jnp.exp2(s_new - m)
    scale = 1.0 / (jnp.sum(lsum, axis=1, keepdims=True) + p_new)
    for p in range(n_pages):
        w_sc[:, p * page:(p + 1) * page] = w_sc[:, p * page:(p + 1) * page] * scale
    _dec_values(w_sc, vt_refs, acc_sc, page)
    _dec_output(acc_sc, p_new * scale, vn_ref, eh, o_ref)


def _decode_attention(layer, page_table, q, kn, vn, cache_k, cache_v, cache_lf=None, lfn=None):
    nb, d = q.shape
    n_pages = page_table.shape[1]
    page = cache_k.shape[2]
    assert page == LANE
    kt = jnp.transpose(cache_k, (0, 1, 3, 4, 2))
    vt = jnp.transpose(cache_v, (0, 1, 3, 4, 2))
    col_head = np.arange(d) // HEAD_DIM
    eh = jnp.asarray((np.arange(N_HEADS)[:, None] == col_head[None, :]).astype(np.float32), BF16)
    tri = jnp.asarray(np.tril(np.ones((page, page), np.float32), -1), BF16)
    qb = jnp.broadcast_to(q.reshape(nb, N_HEADS, HEAD_DIM, 1), (nb, N_HEADS, HEAD_DIM, LANE))
    row3 = lambda a: a.reshape(nb, 1, a.shape[1])
    row_spec = pl.BlockSpec((None, 1, d), lambda bi, pt: (bi, 0, 0))
    fixed = lambda shape: pl.BlockSpec(shape, lambda bi, pt: (0, 0))

    def page_spec(p):
        return pl.BlockSpec((None, None, N_HEADS, HEAD_DIM, page),
                            lambda bi, pt: (layer, pt[bi, p], 0, 0, 0))

    def lf_spec(p):
        return pl.BlockSpec((None, None, N_HEADS, page), lambda bi, pt: (layer, pt[bi, p], 0, 0))

    args = [qb, row3(q), row3(kn), row3(vn)]
    in_specs = [pl.BlockSpec((None, N_HEADS, HEAD_DIM, LANE), lambda bi, pt: (bi, 0, 0, 0)),
                row_spec, row_spec, row_spec]
    if cache_lf is not None:
        args.append(jnp.broadcast_to(lfn[:, :, None], (nb, N_HEADS, LANE)))
        in_specs.append(pl.BlockSpec((None, N_HEADS, LANE), lambda bi, pt: (bi, 0, 0)))
    args += [eh, tri]
    in_specs += [fixed((N_HEADS, d)), fixed((page, page))]
    args += [kt] * n_pages + [vt] * n_pages
    in_specs += [page_spec(p) for p in range(n_pages)] * 2
    if cache_lf is not None:
        args += [jnp.transpose(cache_lf, (0, 1, 3, 2))] * n_pages
        in_specs += [lf_spec(p) for p in range(n_pages)]
        kern = functools.partial(_dec_c_kernel, n_pages=n_pages, page=page)
    else:
        assert n_pages % (MOBA_BLOCK // page) == 0, "past length must be whole MoBA blocks"
        kern = functools.partial(_dec_ab_kernel, n_pages=n_pages, page=page)
    grid_spec = pltpu.PrefetchScalarGridSpec(
        num_scalar_prefetch=1, grid=(nb,), in_specs=in_specs,
        out_specs=pl.BlockSpec((None, 1, d), lambda bi, pt: (bi, 0, 0)),
        scratch_shapes=[pltpu.VMEM((N_HEADS, n_pages * page), F32),
                        pltpu.VMEM((N_HEADS, n_pages * page), F32),
                        pltpu.VMEM((N_HEADS, HEAD_DIM, LANE), F32)])
    out = pl.pallas_call(
        kern, grid_spec=grid_spec, out_shape=jax.ShapeDtypeStruct((nb, 1, d), BF16),
        compiler_params=_params(1, DEC_VMEM),
        name="dec_c" if cache_lf is not None else "dec_ab")(page_table, *args)
    return out.reshape(nb, d)


def _rope_tables(pos):
    half = ROT_DIM // 2
    inv = ROPE_THETA ** (-jnp.arange(half, dtype=F32) * 2.0 / ROT_DIM)
    ang = pos.astype(F32)[:, None] * inv[None, :]
    cos, sin = jnp.cos(ang), jnp.sin(ang)
    rest = jnp.zeros((pos.shape[0], HEAD_DIM - ROT_DIM), F32)
    zero = jnp.zeros_like(sin)
    c = jnp.concatenate([cos, cos, rest + 1.0], axis=1)
    s1 = jnp.concatenate([-sin, zero, rest], axis=1)
    s2 = jnp.concatenate([zero, sin, rest], axis=1)
    return tuple(jnp.concatenate([a, a], axis=1) for a in (c, s1, s2))


def kernel(x_prompt, x_sample, cache_k_ab, cache_v_ab, cache_k_c, cache_v_c, cache_logf_c, state_conv,
           page_table, w_in_ab, w_o_ab, w_in_c, b_f_c, w_o_c, g_mix, g_ffn, w_up, conv_w, conv_b,
           w_down, g_final):
    b, t, d = x_prompt.shape
    nb = x_sample.shape[0]
    assert x_sample.shape[1] == 1 and d == N_HEADS * HEAD_DIM
    depth = g_mix.shape[0]
    past_len = page_table.shape[1] * cache_k_ab.shape[2]
    m = b * t
    tm = min(512, t)
    tq = min(512, t)
    tf = min(512, t)
    cw = 256
    half = d // 2

    xp = x_prompt.reshape(m, d)
    xs = x_sample.reshape(nb, d)
    tabs_p = _rope_tables(jnp.arange(t, dtype=jnp.int32))
    tabs_s = _rope_tables(jnp.full((nb,), past_len, jnp.int32))

    outs = {k: [] for k in ("k_ab_p", "v_ab_p", "k_ab_s", "v_ab_s", "k_c_p", "v_c_p", "f_c_p",
                            "k_c_s", "v_c_s", "f_c_s", "conv_p", "conv_s")}
    for l in range(depth):
        i = l // 2
        gm = g_mix[l].reshape(1, d)
        if l % 2 == 0:
            w_in, w_o = w_in_ab[i].astype(BF16), w_o_ab[i].astype(BF16)
            wq, wk, wv = w_in[:, :d], w_in[:, d:2 * d], w_in[:, 2 * d:3 * d]
            (qb,) = _proj(xp, gm, wq, tm=tm, rope_tabs=tabs_p, emit_f32=False, out_scale=Q_SCALE)
            kf, kb = _proj(xp, gm, wk, tm=tm, rope_tabs=tabs_p)
            vf, vb = _proj(xp, gm, wv, tm=tm)
            n_hp = H_SB * HEAD_DIM // LANE
            o_sb = _sb_attention(qb, kb, vb, b=b, t=t, tq=tq, n_hp=n_hp)
            km = _kmean(kf, col_block=1, width=half).reshape(b, t // MOBA_BLOCK, half)
            km = jnp.pad(km, ((0, 0), (0, LANE - t // MOBA_BLOCK), (0, 0)))
            o_mb = _moba_attention(qb, kb, vb, km, b=b, t=t, tq=tq, n_hp=n_hp, hp0=n_hp)
            xp = _oproj(xp, [o_sb, o_mb], [w_o[:half], w_o[half:]], tm=tm)
            outs["k_ab_p"].append(kf.reshape(b, t, N_HEADS, HEAD_DIM))
            outs["v_ab_p"].append(vf.reshape(b, t, N_HEADS, HEAD_DIM))
            (qs,) = _proj(xs, gm, wq, tm=nb, rope_tabs=tabs_s, emit_f32=False, out_scale=Q_SCALE)
            ksf, _ = _proj(xs, gm, wk, tm=nb, rope_tabs=tabs_s)
            vsf, _ = _proj(xs, gm, wv, tm=nb)
            o_s = _decode_attention(i, page_table, qs.astype(F32), ksf, vsf, cache_k_ab, cache_v_ab)
            xs = _oproj(xs, [o_s], [w_o], tm=nb)
            outs["k_ab_s"].append(ksf.reshape(nb, 1, N_HEADS, HEAD_DIM))
            outs["v_ab_s"].append(vsf.reshape(nb, 1, N_HEADS, HEAD_DIM))
        else:
            w_in, w_o = w_in_c[i].astype(BF16), w_o_c[i].astype(BF16)
            wq, wk, wv = w_in[:, :d], w_in[:, d:2 * d], w_in[:, 2 * d:3 * d]
            wf = jnp.pad(w_in[:, 3 * d:], ((0, 0), (0, LANE - N_HEADS)))
            bf = jnp.pad(b_f_c[i], (0, LANE - N_HEADS)).reshape(1, LANE)
            qb, lf = _proj(xp, gm, wq, tm=tm, gate=(wf, bf), emit_f32=False, out_scale=Q_SCALE)
            kf, kb = _proj(xp, gm, wk, tm=tm)
            vf, vb = _proj(xp, gm, wv, tm=tm)
            logf = lf[:, :N_HEADS].reshape(b, t, N_HEADS)
            pieces = _cumsum_time(jnp.swapaxes(logf, 1, 2), tc=tq)
            ka = jnp.stack(pieces, axis=-1).reshape(b, N_HEADS // 2, 2, t, 3)
            ka = ka.transpose(0, 1, 3, 2, 4).reshape(b, N_HEADS // 2, t, 6)
            ka = jnp.pad(ka, ((0, 0), (0, 0), (0, 0), (0, LANE - 6)))
            o_c = _fox_attention(qb, kb, vb, ka, b=b, t=t, tq=tq)
            xp = _oproj(xp, [o_c], [w_o], tm=tm)
            outs["k_c_p"].append(kf.reshape(b, t, N_HEADS, HEAD_DIM))
            outs["v_c_p"].append(vf.reshape(b, t, N_HEADS, HEAD_DIM))
            outs["f_c_p"].append(logf)
            qs, lfs = _proj(xs, gm, wq, tm=nb, gate=(wf, bf), emit_f32=False, out_scale=Q_SCALE)
            ksf, _ = _proj(xs, gm, wk, tm=nb)
            vsf, _ = _proj(xs, gm, wv, tm=nb)
            o_s = _decode_attention(i, page_table, qs.astype(F32), ksf, vsf, cache_k_c, cache_v_c,
                                    cache_logf_c, lfs[:, :N_HEADS])
            xs = _oproj(xs, [o_s], [w_o], tm=nb)
            outs["k_c_s"].append(ksf.reshape(nb, 1, N_HEADS, HEAD_DIM))
            outs["v_c_s"].append(vsf.reshape(nb, 1, N_HEADS, HEAD_DIM))
            outs["f_c_s"].append(lfs[:, :N_HEADS].reshape(nb, 1, N_HEADS))

        gf = g_ffn[l].reshape(1, d)
        wu, wd = w_up[l].astype(BF16), w_down[l].astype(BF16)
        cb = conv_b[l].reshape(1, -1)
        yp, conv_p = _ffn_prompt(xp.reshape(b, t, d), gf, wu, conv_w[l], cb, wd, tm=tf, cw=cw)
        xp = yp.reshape(m, d)
        xs, conv_s = _ffn_sample(xs, gf, wu, conv_w[l], cb, wd, state_conv[l], cw=cw)
        outs["conv_p"].append(conv_p)
        outs["conv_s"].append(conv_s)

    gfin = g_final.reshape(1, d)
    y_prompt = _final_norm(xp, gfin, tm=tm).reshape(b, t, d)
    y_sample = _final_norm(xs, gfin, tm=nb).reshape(nb, 1, d)
    st = lambda key: jnp.stack(outs[key])
    return (y_prompt, y_sample, st("k_ab_p"), st("v_ab_p"), st("k_ab_s"), st("v_ab_s"),
            st("k_c_p"), st("v_c_p"), st("f_c_p"), st("k_c_s"), st("v_c_s"), st("f_c_s"),
            st("conv_p"), st("conv_s"))
```

```python
import functools

import numpy as np
import jax
import jax.numpy as jnp
from jax import lax
from jax.experimental import pallas as pl
from jax.experimental.pallas import tpu as pltpu

F32 = jnp.float32
BF16 = jnp.bfloat16

HEAD_DIM = 64
N_HEADS = 16
H_SB = N_HEADS // 2
ROT_DIM = HEAD_DIM // 4
ROPE_THETA = 500000.0
MOBA_BLOCK = 256
MOBA_TOPK = 3
RMS_EPS = 1e-6
ATTN_SCALE = HEAD_DIM ** -0.5
CONV_WIDTH = 3
LOG2E = 1.4426950408889634
Q_SCALE = ATTN_SCALE * LOG2E

LANE = 128
SUBLANE = 8
NEG = -1e30
EXIT_LOG2 = -160.0
PROJ_VMEM = 48 * 1024 * 1024
DEC_VMEM = 56 * 1024 * 1024

_NT = (((1,), (1,)), ((), ()))


def _params(n_axes, vmem=PROJ_VMEM):
    return pltpu.CompilerParams(dimension_semantics=("arbitrary",) * n_axes,
                                vmem_limit_bytes=vmem)


def _rms(x, g):
    ms = jnp.mean(x * x, axis=-1, keepdims=True)
    return x * lax.rsqrt(ms + RMS_EPS) * g


def _log_sigmoid(z):
    return jnp.minimum(z, 0.0) - jnp.log1p(jnp.exp(-jnp.abs(z)))


def _log2_sigmoid(z2):
    return jnp.minimum(z2, 0.0) - jnp.log2(1.0 + jnp.exp2(-jnp.abs(z2)))


def _split2(x):
    hi = x.astype(BF16)
    lo = (x - hi.astype(F32)).astype(BF16)
    return hi, lo


def _proj_kernel(*refs, rope, emit_f32, gate, out_scale, emit_t, aliased):
    it = iter(refs)
    x_ref, g_ref, w_ref = next(it), next(it), next(it)
    if rope:
        c_ref, s1_ref, s2_ref = next(it), next(it), next(it)
    if gate:
        wf_ref, bf_ref = next(it), next(it)
    if emit_t:
        wt_ref = next(it)
        if rope:
            ct_ref, st_ref = next(it), next(it)
        if aliased:
            next(it)
    if emit_f32:
        of_ref = next(it)
    ob_ref = next(it)
    if gate:
        og_ref = next(it)
    if emit_t:
        ot_ref = next(it)

    xn = _rms(x_ref[...], g_ref[...]).astype(BF16)
    y = jnp.dot(xn, w_ref[...], preferred_element_type=F32)
    if rope:
        half = y.shape[1] // 2
        c, s1, s2 = c_ref[...], s1_ref[...], s2_ref[...]
        parts = [y[:, :half]]
        for ch in range(half // LANE):
            yc = y[:, half + ch * LANE: half + (ch + 1) * LANE]
            up = pltpu.roll(yc, LANE - ROT_DIM // 2, 1)
            dn = pltpu.roll(yc, ROT_DIM // 2, 1)
            parts.append(yc * c + up * s1 + dn * s2)
        y = jnp.concatenate(parts, axis=1)
    if emit_f32:
        of_ref[...] = y
    ob_ref[...] = (y * out_scale if out_scale != 1.0 else y).astype(BF16)
    if gate:
        f = jnp.dot(xn, wf_ref[...], preferred_element_type=F32) + bf_ref[...]
        og_ref[...] = _log_sigmoid(f)
    if emit_t:
        yt = lax.dot_general(wt_ref[...], xn, _NT, preferred_element_type=F32)
        if rope:
            half = yt.shape[0] // 2
            rot = yt[half:].reshape(half // HEAD_DIM, HEAD_DIM, yt.shape[1])
            x1, x2 = rot[:, 0:ROT_DIM // 2], rot[:, ROT_DIM // 2:ROT_DIM]
            ct, st = ct_ref[...], st_ref[...]
            rot = jnp.concatenate([x1 * ct - x2 * st, x1 * st + x2 * ct, rot[:, ROT_DIM:]], axis=1)
            yt = jnp.concatenate([yt[:half], rot.reshape(half, yt.shape[1])], axis=0)
        ot_ref[...] = yt


def _proj(x, g, w, *, tm, rope_tabs=None, gate=None, emit_f32=True, out_scale=1.0, t_out=None):
    m, d = x.shape
    n = w.shape[1]
    row = lambda i: (i, 0)
    fixed = lambda i: (0, 0)
    in_specs = [pl.BlockSpec((tm, d), row), pl.BlockSpec((1, d), fixed), pl.BlockSpec((d, n), fixed)]
    args = [x, g, w]
    if rope_tabs is not None:
        n_tab_blocks = rope_tabs[0].shape[0] // tm
        tab = lambda i: (i % n_tab_blocks, 0)
        in_specs += [pl.BlockSpec((tm, LANE), tab)] * 3
        args += list(rope_tabs)
    if gate is not None:
        in_specs += [pl.BlockSpec((d, LANE), fixed), pl.BlockSpec((1, LANE), fixed)]
        args += list(gate)
    aliases = {}
    if t_out is not None:
        n_t = m // t_out["batch"] // tm
        in_specs.append(pl.BlockSpec((n, d), fixed))
        args.append(t_out["wt"])
        if rope_tabs is not None:
            in_specs += [pl.BlockSpec((ROT_DIM // 2, tm), lambda i: (0, i % n_t))] * 2
            args += list(t_out["tabs_t"])
        if t_out["prev"] is not None:
            aliases[len(args)] = int(emit_f32) + 1 + int(gate is not None)
            in_specs.append(pl.BlockSpec(memory_space=pl.ANY))
            args.append(t_out["prev"])
    out_shape, out_specs = [], []
    if emit_f32:
        out_shape.append(jax.ShapeDtypeStruct((m, n), F32))
        out_specs.append(pl.BlockSpec((tm, n), row))
    out_shape.append(jax.ShapeDtypeStruct((m, n), BF16))
    out_specs.append(pl.BlockSpec((tm, n), row))
    if gate is not None:
        out_shape.append(jax.ShapeDtypeStruct((m, LANE), F32))
        out_specs.append(pl.BlockSpec((tm, LANE), row))
    if t_out is not None:
        layer = t_out["layer"]
        out_shape.append(jax.ShapeDtypeStruct((t_out["n_layers"], t_out["batch"], n, m // t_out["batch"]), F32))
        out_specs.append(pl.BlockSpec((None, None, n, tm), lambda i: (layer, i // n_t, 0, i % n_t)))
    kern = functools.partial(_proj_kernel, rope=rope_tabs is not None, emit_f32=emit_f32,
                             gate=gate is not None, out_scale=out_scale, emit_t=t_out is not None,
                             aliased=bool(aliases))
    return pl.pallas_call(kern, grid=(m // tm,), in_specs=in_specs, out_specs=out_specs,
                          out_shape=out_shape, input_output_aliases=aliases,
                          compiler_params=_params(1), name="proj")(*args)


def _oproj_kernel(x_ref, *refs, n_parts):
    o_refs, w_refs, out_ref = refs[:n_parts], refs[n_parts:2 * n_parts], refs[2 * n_parts]
    y = jnp.dot(o_refs[0][...], w_refs[0][...], preferred_element_type=F32)
    for o_ref, w_ref in zip(o_refs[1:], w_refs[1:]):
        y = y + jnp.dot(o_ref[...], w_ref[...], preferred_element_type=F32)
    out_ref[...] = x_ref[...] + y


def _oproj(x, parts, weights, *, tm):
    m, d = x.shape
    row = lambda i: (i, 0)
    fixed = lambda i: (0, 0)
    in_specs = [pl.BlockSpec((tm, d), row)]
    in_specs += [pl.BlockSpec((tm, p.shape[1]), row) for p in parts]
    in_specs += [pl.BlockSpec(w.shape, fixed) for w in weights]
    return pl.pallas_call(
        functools.partial(_oproj_kernel, n_parts=len(parts)), grid=(m // tm,),
        in_specs=in_specs, out_specs=pl.BlockSpec((tm, d), row),
        out_shape=jax.ShapeDtypeStruct((m, d), F32), compiler_params=_params(1),
        name="oproj")(x, *parts, *weights)


def _silu_gate(gt, vl):
    return gt * (1.0 / (1.0 + jnp.exp(-gt))) * vl


def _ffn_prompt_kernel(x_ref, g_ref, wu_ref, cw_ref, cb_ref, wd_ref, out_ref, st_ref,
                       xn_sc, act_sc, ug_sc, uv_sc, carry_sc, *, cw):
    tm = x_ref.shape[0]
    dff = wd_ref.shape[0]

    @pl.when(pl.program_id(1) == 0)
    def _():
        carry_sc[...] = jnp.zeros(carry_sc.shape, F32)

    x = x_ref[...]
    xn_sc[...] = _rms(x, g_ref[...]).astype(BF16)

    def conv(col, u_sc):
        cols = slice(col, col + cw)
        u = jnp.dot(xn_sc[...], wu_ref[:, cols], preferred_element_type=F32)
        u_sc[0:SUBLANE, :] = carry_sc[:, cols]
        u_sc[SUBLANE:SUBLANE + tm, :] = u
        tail = u[tm - SUBLANE:tm, :]
        carry_sc[:, cols] = tail
        st_ref[:, cols] = tail
        w = cw_ref[:, cols]
        u0 = u_sc[SUBLANE - 2:SUBLANE - 2 + tm, :]
        u1 = u_sc[SUBLANE - 1:SUBLANE - 1 + tm, :]
        return ((cb_ref[:, cols] + w[0:1] * u0) + w[1:2] * u1) + w[2:3] * u

    for c in range(dff // cw):
        gt = conv(c * cw, ug_sc)
        vl = conv(dff + c * cw, uv_sc)
        act_sc[:, c * cw:(c + 1) * cw] = _silu_gate(gt, vl).astype(BF16)
    out_ref[...] = x + jnp.dot(act_sc[...], wd_ref[...], preferred_element_type=F32)


def _ffn_prompt(x, g, w_up, conv_w, conv_b, w_down, *, tm, cw):
    b, t, d = x.shape
    dff = w_down.shape[0]
    xmap = lambda bi, i: (bi, i, 0)
    resident = lambda shape: pl.BlockSpec(shape, lambda bi, i: (0, 0), pipeline_mode=pl.Buffered(1))
    in_specs = [pl.BlockSpec((None, tm, d), xmap), resident((1, d)), resident((d, 2 * dff)),
                resident((CONV_WIDTH, 2 * dff)), resident((1, 2 * dff)), resident((dff, d))]
    out_specs = [pl.BlockSpec((None, tm, d), xmap),
                 pl.BlockSpec((None, None, SUBLANE, 2 * dff), lambda bi, i: (bi, i, 0, 0))]
    out_shape = [jax.ShapeDtypeStruct((b, t, d), F32),
                 jax.ShapeDtypeStruct((b, t // tm, SUBLANE, 2 * dff), F32)]
    scratch = [pltpu.VMEM((tm, d), BF16), pltpu.VMEM((tm, dff), BF16),
               pltpu.VMEM((tm + SUBLANE, cw), F32), pltpu.VMEM((tm + SUBLANE, cw), F32),
               pltpu.VMEM((SUBLANE, 2 * dff), F32)]
    y, st = pl.pallas_call(
        functools.partial(_ffn_prompt_kernel, cw=cw), grid=(b, t // tm), in_specs=in_specs,
        out_specs=out_specs, out_shape=out_shape, scratch_shapes=scratch,
        compiler_params=_params(2, DEC_VMEM), name="ffn_prompt")(x, g, w_up, conv_w, conv_b, w_down)
    return y, st[:, -1, SUBLANE - (CONV_WIDTH - 1):, :]


def _ffn_sample_kernel(x_ref, g_ref, wg_ref, wv_ref, cwg_ref, cwv_ref, cbg_ref, cbv_ref, wd_ref,
                       s0g_ref, s0v_ref, s1g_ref, s1v_ref, out_ref, ug_ref, uv_ref, xn_sc, acc_sc):
    c = pl.program_id(0)

    @pl.when(c == 0)
    def _():
        x = x_ref[...]
        xn_sc[...] = _rms(x, g_ref[...]).astype(BF16)
        acc_sc[...] = x

    xn = xn_sc[...]

    def conv(w_ref, cw_ref, cb_ref, s0_ref, s1_ref, u_ref):
        u = jnp.dot(xn, w_ref[...], preferred_element_type=F32)
        u_ref[...] = u
        cw = cw_ref[...]
        return ((cb_ref[...] + cw[0:1] * s0_ref[...]) + cw[1:2] * s1_ref[...]) + cw[2:3] * u

    gt = conv(wg_ref, cwg_ref, cbg_ref, s0g_ref, s1g_ref, ug_ref)
    vl = conv(wv_ref, cwv_ref, cbv_ref, s0v_ref, s1v_ref, uv_ref)
    act = _silu_gate(gt, vl).astype(BF16)
    acc_sc[...] += jnp.dot(act, wd_ref[...], preferred_element_type=F32)

    @pl.when(c == pl.num_programs(0) - 1)
    def _():
        out_ref[...] = acc_sc[...]


def _ffn_sample(x, g, w_up, conv_w, conv_b, w_down, state, *, cw):
    m, d = x.shape
    dff = w_down.shape[0]
    nc = dff // cw
    st = state.reshape(m, (CONV_WIDTH - 1) * 2 * dff)
    fixed = lambda c: (0, 0)
    in_specs = [
        pl.BlockSpec((m, d), fixed),
        pl.BlockSpec((1, d), fixed),
        pl.BlockSpec((d, cw), lambda c: (0, c)),
        pl.BlockSpec((d, cw), lambda c: (0, c + nc)),
        pl.BlockSpec((CONV_WIDTH, cw), lambda c: (0, c)),
        pl.BlockSpec((CONV_WIDTH, cw), lambda c: (0, c + nc)),
        pl.BlockSpec((1, cw), lambda c: (0, c)),
        pl.BlockSpec((1, cw), lambda c: (0, c + nc)),
        pl.BlockSpec((cw, d), lambda c: (c, 0)),
        pl.BlockSpec((m, cw), lambda c: (0, c)),
        pl.BlockSpec((m, cw), lambda c: (0, c + nc)),
        pl.BlockSpec((m, cw), lambda c: (0, c + 2 * nc)),
        pl.BlockSpec((m, cw), lambda c: (0, c + 3 * nc)),
    ]
    out_specs = [pl.BlockSpec((m, d), fixed),
                 pl.BlockSpec((m, cw), lambda c: (0, c)),
                 pl.BlockSpec((m, cw), lambda c: (0, c))]
    out_shape = [jax.ShapeDtypeStruct((m, d), F32),
                 jax.ShapeDtypeStruct((m, dff), F32),
                 jax.ShapeDtypeStruct((m, dff), F32)]
    scratch = [pltpu.VMEM((m, d), BF16), pltpu.VMEM((m, d), F32)]
    y, ug, uv = pl.pallas_call(
        _ffn_sample_kernel, grid=(nc,), in_specs=in_specs, out_specs=out_specs,
        out_shape=out_shape, scratch_shapes=scratch, compiler_params=_params(1),
        name="ffn_sample")(x, g, w_up, w_up, conv_w, conv_w, conv_b, conv_b, w_down, st, st, st, st)
    u = jnp.concatenate([ug, uv], axis=-1)
    new_state = jnp.stack([state[:, 1, :], u], axis=1)
    return y, new_state


def _norm_kernel(x_ref, g_ref, o_ref):
    o_ref[...] = _rms(x_ref[...], g_ref[...])


def _final_norm(x, g, *, tm):
    m, d = x.shape
    return pl.pallas_call(
        _norm_kernel, grid=(m // tm,),
        in_specs=[pl.BlockSpec((tm, d), lambda i: (i, 0)), pl.BlockSpec((1, d), lambda i: (0, 0))],
        out_specs=pl.BlockSpec((tm, d), lambda i: (i, 0)),
        out_shape=jax.ShapeDtypeStruct((m, d), F32), compiler_params=_params(1),
        name="final_norm")(x, g)


def _split3(x):
    hi = x.astype(BF16)
    r1 = x - hi.astype(F32)
    mid = r1.astype(BF16)
    lo = (r1 - mid.astype(F32)).astype(BF16)
    return hi, mid, lo


def _cumsum_kernel(x_ref, u_ref, hi_ref, mid_ref, lo_ref, carry_sc):
    @pl.when(pl.program_id(1) == 0)
    def _():
        carry_sc[...] = jnp.zeros_like(carry_sc)

    u = u_ref[...]
    hi, mid, lo = _split3(x_ref[...])
    c = (jnp.dot(hi, u, preferred_element_type=F32) + jnp.dot(mid, u, preferred_element_type=F32)
         + jnp.dot(lo, u, preferred_element_type=F32)) + carry_sc[...]
    carry_sc[...] = c[:, c.shape[1] - 1:]
    hi_ref[...], mid_ref[...], lo_ref[...] = _split3(c * (-LOG2E))


def _cumsum_time(logf_t, *, tc):
    b, h, t = logf_t.shape
    upper = jnp.asarray(np.triu(np.ones((tc, tc), np.float32)), BF16)
    spec = pl.BlockSpec((None, h, tc), lambda bi, i: (bi, 0, i))
    return pl.pallas_call(
        _cumsum_kernel, grid=(b, t // tc),
        in_specs=[spec, pl.BlockSpec((tc, tc), lambda bi, i: (0, 0))],
        out_specs=[spec] * 3,
        out_shape=[jax.ShapeDtypeStruct((b, h, t), BF16)] * 3,
        scratch_shapes=[pltpu.VMEM((h, 1), F32)], compiler_params=_params(2),
        name="cumsum")(logf_t, upper)


def _kmean_kernel(k_ref, o_ref):
    o_ref[...] = jnp.mean(k_ref[...], axis=0, keepdims=True)


def _kmean(k_flat, *, col_block, width):
    m = k_flat.shape[0]
    nblk = m // MOBA_BLOCK
    out = pl.pallas_call(
        _kmean_kernel, grid=(nblk,),
        in_specs=[pl.BlockSpec((MOBA_BLOCK, width), lambda n: (n, col_block))],
        out_specs=pl.BlockSpec((None, 1, width), lambda n: (n, 0, 0)),
        out_shape=jax.ShapeDtypeStruct((nblk, 1, width), F32), compiler_params=_params(1),
        name="kmean")(k_flat)
    return out.reshape(nblk, width)


def _stack_heads(q2, tq):
    lane = lax.broadcasted_iota(jnp.int32, (tq, LANE), 1)
    zero = jnp.zeros_like(q2)
    return jnp.where(lane < HEAD_DIM, q2, zero), jnp.where(lane >= HEAD_DIM, q2, zero)


def _unstack_heads(o, tq):
    lane = lax.broadcasted_iota(jnp.int32, (tq, LANE), 1)
    return jnp.where(lane < HEAD_DIM, o[:tq], o[tq:])


def _causal_keep(tq, tk, strict):
    r = lax.broadcasted_iota(jnp.int32, (2 * tq, tk), 0)
    r = jnp.where(r >= tq, r - tq, r)
    c = lax.broadcasted_iota(jnp.int32, (2 * tq, tk), 1)
    return (c < r) if strict else (c <= r)


def _softmax_step(s, v2, m_sc, acc_sc):
    tq = s.shape[0] // 2
    m_prev = m_sc[...]
    m_new = jnp.maximum(m_prev, jnp.max(s, axis=1, keepdims=True))
    alpha = jnp.exp2(m_prev - m_new)
    p = jnp.exp2(s - jnp.concatenate([m_new] * (s.shape[1] // LANE), axis=1)).astype(BF16)
    lane = lax.broadcasted_iota(jnp.int32, v2.shape, 1)
    one = jnp.ones_like(v2)
    pv = jnp.concatenate(
        [jnp.dot(p[:tq], jnp.where(lane < HEAD_DIM, v2, one), preferred_element_type=F32),
         jnp.dot(p[tq:], jnp.where(lane >= HEAD_DIM, v2, one), preferred_element_type=F32)], axis=0)
    acc_sc[...] = alpha * acc_sc[...] + pv
    m_sc[...] = m_new


def _softmax_finish(acc_sc, tq):
    acc = acc_sc[...]
    o = acc / pltpu.roll(acc, HEAD_DIM, 1)
    return _unstack_heads(o, tq).astype(BF16)


def _flash_loop(qi, tq, lhs_sc, k_ref, aug_ref, v_ref, m_sc, acc_sc, negligible=None):
    m_sc[...] = jnp.full(m_sc.shape, NEG, F32)
    acc_sc[...] = jnp.zeros(acc_sc.shape, F32)

    def chunk(j, diag):
        off = pl.multiple_of(j * tq, tq)
        kaug = jnp.concatenate([k_ref[pl.ds(off, tq), :], aug_ref[pl.ds(off, tq), :]], axis=1)
        s = lax.dot_general(lhs_sc[...], kaug, _NT, preferred_element_type=F32)
        if diag:
            s = jnp.where(_causal_keep(tq, tq, False), s, NEG)
        _softmax_step(s, v_ref[pl.ds(off, tq), :], m_sc, acc_sc)

    chunk(qi, True)

    def cond(jj):
        more = jj < qi
        if negligible is None:
            return more
        return jnp.logical_and(more, jnp.logical_not(negligible(jnp.maximum(qi - 1 - jj, 0))))

    def body(jj):
        chunk(qi - 1 - jj, False)
        return jj + 1

    lax.while_loop(cond, body, jnp.int32(0))


def _fox_kernel(q_ref, k_ref, v_ref, ka_ref, o_ref, lhs_sc, m_sc, acc_sc, ub_sc, kmax_sc, bmax_sc,
                *, tq):
    qi = pl.program_id(2)
    n_pieces = 3
    qa, qb = _stack_heads(q_ref[...], tq)
    lhs_sc[0:tq, 0:LANE] = qa
    lhs_sc[tq:2 * tq, 0:LANE] = qb
    lane = lax.broadcasted_iota(jnp.int32, (2 * tq, LANE), 1)
    row = lax.broadcasted_iota(jnp.int32, (2 * tq, LANE), 0)
    piece_head = jnp.where(lane < n_pieces, 0, jnp.where(lane < 2 * n_pieces, 1, 2))
    lhs_sc[:, LANE:2 * LANE] = jnp.where(piece_head == jnp.where(row >= tq, 1, 0), 1.0, 0.0).astype(BF16)

    @pl.when(qi == 0)
    def _():
        def kbody(j, acc):
            blk = k_ref[pl.ds(pl.multiple_of(j * tq, tq), tq), :].astype(F32)
            return jnp.maximum(acc, jnp.max(jnp.abs(blk), axis=0, keepdims=True))

        n_chunks = k_ref.shape[0] // tq
        kabs = lax.fori_loop(0, n_chunks, kbody, jnp.zeros((1, LANE), F32))
        lane1 = lax.broadcasted_iota(jnp.int32, (1, LANE), 1)
        kmax_sc[0] = jnp.max(jnp.where(lane1 < HEAD_DIM, kabs, 0.0))
        kmax_sc[1] = jnp.max(jnp.where(lane1 >= HEAD_DIM, kabs, 0.0))

        def bbody(j, carry):
            off = pl.multiple_of(j * tq + tq - 2 * SUBLANE, 2 * SUBLANE)
            tail = ka_ref[pl.ds(off, 2 * SUBLANE), :].astype(F32)
            r16 = lax.broadcasted_iota(jnp.int32, tail.shape, 0)
            l16 = lax.broadcasted_iota(jnp.int32, tail.shape, 1)
            last = jnp.where(r16 == 2 * SUBLANE - 1, tail, 0.0)
            bmax_sc[2 * j] = jnp.sum(jnp.where(l16 < n_pieces, last, 0.0))
            bmax_sc[2 * j + 1] = jnp.sum(
                jnp.where(l16 < n_pieces, 0.0, jnp.where(l16 < 2 * n_pieces, last, 0.0)))
            return carry

        lax.fori_loop(0, n_chunks, bbody, 0)

    q1 = jnp.sum(jnp.abs(lhs_sc[:, 0:LANE].astype(F32)), axis=1, keepdims=True)
    ub_sc[...] = jnp.broadcast_to(q1, (2 * tq, LANE)) * jnp.where(row >= tq, kmax_sc[1], kmax_sc[0])

    def negligible(j):
        gap = ub_sc[...] + jnp.where(row >= tq, bmax_sc[2 * j + 1], bmax_sc[2 * j]) - m_sc[...]
        return jnp.max(gap) < EXIT_LOG2

    _flash_loop(qi, tq, lhs_sc, k_ref, ka_ref, v_ref, m_sc, acc_sc, negligible)
    o_ref[...] = _softmax_finish(acc_sc, tq)


def _flash_scratch(tq):
    return [pltpu.VMEM((2 * tq, 2 * LANE), BF16), pltpu.VMEM((2 * tq, LANE), F32),
            pltpu.VMEM((2 * tq, LANE), F32)]


def _fox_attention(q, k, v, ka, *, b, t, tq):
    m = q.shape[0]
    n_hp = q.shape[1] // LANE
    nq = t // tq
    return pl.pallas_call(
        functools.partial(_fox_kernel, tq=tq), grid=(b, n_hp, nq),
        in_specs=[pl.BlockSpec((tq, LANE), lambda bi, hp, qi: (bi * nq + qi, hp)),
                  pl.BlockSpec((t, LANE), lambda bi, hp, qi: (bi, hp)),
                  pl.BlockSpec((t, LANE), lambda bi, hp, qi: (bi, hp)),
                  pl.BlockSpec((None, None, t, LANE), lambda bi, hp, qi: (bi, hp, 0, 0))],
        out_specs=pl.BlockSpec((tq, LANE), lambda bi, hp, qi: (bi * nq + qi, hp)),
        out_shape=jax.ShapeDtypeStruct((m, n_hp * LANE), BF16),
        scratch_shapes=_flash_scratch(tq) + [pltpu.VMEM((2 * tq, LANE), F32), pltpu.SMEM((2,), F32),
                                             pltpu.SMEM((2 * nq,), F32)],
        compiler_params=_params(3), name="fox_attn")(q, k, v, ka)


def _sb_kernel(q_ref, k_ref, v_ref, u_ref, o_ref, qs_sc, r_sc, acc_sc, *, tq, sub):
    qi = pl.program_id(2)
    qa, qb = _stack_heads(q_ref[...], tq)
    qs_sc[0:tq, :] = qa
    qs_sc[tq:2 * tq, :] = qb
    r_sc[...] = jnp.zeros(r_sc.shape, F32)
    acc_sc[...] = jnp.zeros(acc_sc.shape, F32)
    u = u_ref[...]

    def chunk(j, diag):
        off = pl.multiple_of(j * tq, tq)
        v2 = v_ref[pl.ds(off, tq), :]
        z_all = lax.dot_general(qs_sc[...], k_ref[pl.ds(off, tq), :], _NT, preferred_element_type=F32)
        keep_all = _causal_keep(tq, tq, True) if diag else None
        for sb in reversed(range(tq // sub)):
            z = z_all[:, sb * sub:(sb + 1) * sub]
            neg_abs = lax.bitcast_convert_type(
                lax.bitcast_convert_type(z, jnp.uint32) | jnp.uint32(0x80000000), F32)
            sp = jnp.maximum(z, 0.0) + jnp.log2(1.0 + jnp.exp2(neg_abs))
            if diag:
                keep = keep_all[:, sb * sub:(sb + 1) * sub]
                sp = jnp.where(keep, sp, 0.0)
            cum = jnp.dot(sp.astype(BF16), u, preferred_element_type=F32)
            r = r_sc[...]
            w = jnp.exp2((z - sp) + (cum + jnp.concatenate([r] * (sub // LANE), axis=1)))
            if diag:
                w = jnp.where(keep, w, 0.0)
            acc_sc[...] += jnp.dot(w.astype(BF16), v2[sb * sub:(sb + 1) * sub, :],
                                   preferred_element_type=F32)
            r_sc[...] = r + (cum[:, 0:1] - sp[:, 0:1])

    chunk(qi, True)

    def cond(jj):
        return jnp.logical_and(jj < qi, jnp.max(r_sc[...]) >= EXIT_LOG2)

    def body(jj):
        chunk(qi - 1 - jj, False)
        return jj + 1

    lax.while_loop(cond, body, jnp.int32(0))
    o_ref[...] = _unstack_heads(acc_sc[...], tq).astype(BF16)


def _sb_attention(q, k, v, *, b, t, tq, n_hp, sub=256):
    m = q.shape[0]
    nq = t // tq
    lower = jnp.asarray(-np.tril(np.ones((sub, sub), np.float32), -1), BF16)
    scratch = [pltpu.VMEM((2 * tq, LANE), BF16), pltpu.VMEM((2 * tq, LANE), F32),
               pltpu.VMEM((2 * tq, LANE), F32)]
    return pl.pallas_call(
        functools.partial(_sb_kernel, tq=tq, sub=sub), grid=(b, n_hp, nq),
        in_specs=[pl.BlockSpec((tq, LANE), lambda bi, hp, qi: (bi * nq + qi, hp)),
                  pl.BlockSpec((t, LANE), lambda bi, hp, qi: (bi, hp)),
                  pl.BlockSpec((t, LANE), lambda bi, hp, qi: (bi, hp)),
                  pl.BlockSpec((sub, sub), lambda bi, hp, qi: (0, 0))],
        out_specs=pl.BlockSpec((tq, LANE), lambda bi, hp, qi: (bi * nq + qi, hp)),
        out_shape=jax.ShapeDtypeStruct((m, n_hp * LANE), BF16),
        scratch_shapes=scratch, compiler_params=_params(3), name="sb_attn")(q, k, v, lower)


def _moba_kernel(q_ref, k_ref, v_ref, km_ref, et_ref, o_ref, lhs_sc, m_sc, acc_sc, *, tq):
    qi = pl.program_id(2)
    qa, qb = _stack_heads(q_ref[...], tq)
    lhs_sc[0:tq, 0:LANE] = qa
    lhs_sc[tq:2 * tq, 0:LANE] = qb

    qs = lhs_sc[:, 0:LANE]
    km_hi, km_lo = _split2(km_ref[...])
    gate = (lax.dot_general(qs, km_hi, _NT, preferred_element_type=F32)
            + lax.dot_general(qs, km_lo, _NT, preferred_element_type=F32))
    n_idx = lax.broadcasted_iota(jnp.int32, (2 * tq, LANE), 1)
    row = lax.broadcasted_iota(jnp.int32, (2 * tq, LANE), 0)
    pos = qi * tq + jnp.where(row >= tq, row - tq, row)
    q_blk = jnp.right_shift(pos, MOBA_BLOCK.bit_length() - 1)
    n_f = n_idx.astype(F32)
    g = jnp.where(n_idx < q_blk, gate, -jnp.inf)
    bias = jnp.where(n_idx == q_blk, 0.0, NEG)
    for _ in range(MOBA_TOPK):
        mx = jnp.max(g, axis=1, keepdims=True)
        first = jnp.min(jnp.where(g == mx, n_f, float(LANE)), axis=1, keepdims=True)
        hit = n_f == first
        bias = jnp.where(jnp.where(hit, n_idx, q_blk) < q_blk, 0.0, bias)
        g = jnp.where(hit, -jnp.inf, g)
    lhs_sc[:, LANE:2 * LANE] = bias.astype(BF16)
    _flash_loop(qi, tq, lhs_sc, k_ref, et_ref, v_ref, m_sc, acc_sc)
    o_ref[...] = _softmax_finish(acc_sc, tq)


def _moba_attention(q, k, v, kmean_pad, *, b, t, tq, n_hp, hp0):
    m = q.shape[0]
    nq = t // tq
    et = (np.arange(t)[:, None] // MOBA_BLOCK == np.arange(LANE)[None, :]).astype(np.float32)
    et = jnp.asarray(et, BF16)
    scratch = _flash_scratch(tq)
    return pl.pallas_call(
        functools.partial(_moba_kernel, tq=tq), grid=(b, n_hp, nq),
        in_specs=[pl.BlockSpec((tq, LANE), lambda bi, hp, qi: (bi * nq + qi, hp0 + hp)),
                  pl.BlockSpec((t, LANE), lambda bi, hp, qi: (bi, hp0 + hp)),
                  pl.BlockSpec((t, LANE), lambda bi, hp, qi: (bi, hp0 + hp)),
                  pl.BlockSpec((None, LANE, LANE), lambda bi, hp, qi: (bi, 0, hp)),
                  pl.BlockSpec((t, LANE), lambda bi, hp, qi: (0, 0))],
        out_specs=pl.BlockSpec((tq, LANE), lambda bi, hp, qi: (bi * nq + qi, hp)),
        out_shape=jax.ShapeDtypeStruct((m, n_hp * LANE), BF16),
        scratch_shapes=scratch, compiler_params=_params(3), name="moba_attn")(q, k, v, kmean_pad, et)


def _dec_scores(qb_ref, kt_refs, s_sc, page):
    for h in range(N_HEADS):
        qh = qb_ref[h]
        for p, kt_ref in enumerate(kt_refs):
            s_sc[h:h + 1, p * page:(p + 1) * page] = jnp.sum(kt_ref[h] * qh, axis=0, keepdims=True)


def _dec_new_score(qrow_ref, kn_ref, eh):
    prod = jnp.broadcast_to(kn_ref[...] * qrow_ref[...], (LANE, eh.shape[1]))
    hi, lo = _split2(prod)
    return (lax.dot_general(eh, hi, _NT, preferred_element_type=F32)
            + lax.dot_general(eh, lo, _NT, preferred_element_type=F32))


def _lane_suffix(x, tri):
    hi, lo = _split2(x)
    return jnp.dot(hi, tri, preferred_element_type=F32) + jnp.dot(lo, tri, preferred_element_type=F32)


def _dec_values(w_sc, vt_refs, acc_sc, page):
    for h in range(N_HEADS):
        acc = jnp.zeros(acc_sc.shape[1:], F32)
        for p, vt_ref in enumerate(vt_refs):
            acc = acc + vt_ref[h] * w_sc[h:h + 1, p * page:(p + 1) * page]
        acc_sc[h] = acc


def _dec_output(acc_sc, p_new, vn_ref, eh, o_ref):
    d = eh.shape[1]
    hi, lo = _split2(acc_sc[...].reshape(d, LANE))
    ones = jnp.ones((SUBLANE, LANE), BF16)
    o = (lax.dot_general(ones, hi, _NT, preferred_element_type=F32)
         + lax.dot_general(ones, lo, _NT, preferred_element_type=F32))[0:1]
    pn = jnp.sum(jnp.concatenate([p_new] * (d // LANE), axis=1) * eh.astype(F32), axis=0, keepdims=True)
    o_ref[...] = (o + pn * vn_ref[...]).astype(BF16)


def _dec_ab_kernel(pt_ref, qb_ref, qrow_ref, kn_ref, vn_ref, eh_ref, tri_ref, *rest, n_pages, page):
    kt_refs, vt_refs = rest[:n_pages], rest[n_pages:2 * n_pages]
    o_ref, s_sc, w_sc, acc_sc = rest[2 * n_pages:2 * n_pages + 4]
    eh, tri = eh_ref[...], tri_ref[...]
    lane_f = lax.broadcasted_iota(jnp.int32, (N_HEADS, LANE), 1).astype(F32)
    is_sb = lax.broadcasted_iota(jnp.int32, (N_HEADS, LANE), 0) < H_SB

    _dec_scores(qb_ref, kt_refs, s_sc, page)
    s_new = _dec_new_score(qrow_ref, kn_ref, eh)

    ppb = MOBA_BLOCK // page
    nblk = n_pages // ppb
    g = jnp.full((N_HEADS, LANE), -jnp.inf, F32)
    for n in range(nblk):
        gn = jnp.sum(s_sc[:, n * MOBA_BLOCK:(n + 1) * MOBA_BLOCK], axis=1, keepdims=True)
        g = jnp.where(lane_f == float(n), gn, g)
    firsts = []
    for _ in range(min(MOBA_TOPK, nblk + 1)):
        mx = jnp.max(g, axis=1, keepdims=True)
        first = jnp.min(jnp.where(g == mx, lane_f, float(LANE)), axis=1, keepdims=True)
        firsts.append(first)
        g = jnp.where(lane_f == first, -jnp.inf, g)
    sel_bias = []
    for n in range(nblk):
        miss = functools.reduce(jnp.minimum, [jnp.abs(f - float(n)) for f in firsts])
        sel_bias.append(jnp.where(miss == 0.0, 0.0, NEG))
    m = s_new
    for p in range(n_pages):
        sc = s_sc[:, p * page:(p + 1) * page] + sel_bias[p // ppb]
        m = jnp.maximum(m, jnp.max(sc, axis=1, keepdims=True))

    carry = jnp.zeros((N_HEADS, LANE), F32)
    lsum = jnp.zeros((N_HEADS, LANE), F32)
    for p in reversed(range(n_pages)):
        z = s_sc[:, p * page:(p + 1) * page]
        ls = _log2_sigmoid(z)
        l1m = ls - z
        cum = _lane_suffix(l1m, tri)
        w_sb = jnp.exp2(ls + (cum + carry))
        carry = carry + (cum[:, 0:1] + l1m[:, 0:1])
        pm = jnp.exp2((z + sel_bias[p // ppb]) - m)
        lsum = lsum + pm
        w_sc[:, p * page:(p + 1) * page] = jnp.where(is_sb, w_sb, pm)
    p_new = jnp.where(is_sb, 0.0, jnp.exp2(s_new - m))
    scale = jnp.where(is_sb, 1.0, 1.0 / (jnp.sum(lsum, axis=1, keepdims=True) + p_new))
    for p in range(n_pages):
        w_sc[:, p * page:(p + 1) * page] = w_sc[:, p * page:(p + 1) * page] * scale
    _dec_values(w_sc, vt_refs, acc_sc, page)
    _dec_output(acc_sc, p_new * scale, vn_ref, eh, o_ref)


def _dec_c_kernel(pt_ref, qb_ref, qrow_ref, kn_ref, vn_ref, lfn_ref, eh_ref, tri_ref, *rest,
                  n_pages, page):
    kt_refs, vt_refs = rest[:n_pages], rest[n_pages:2 * n_pages]
    lf_refs = rest[2 * n_pages:3 * n_pages]
    o_ref, s_sc, w_sc, acc_sc = rest[3 * n_pages:3 * n_pages + 4]
    eh, tri = eh_ref[...], tri_ref[...]

    _dec_scores(qb_ref, kt_refs, s_sc, page)
    s_new = _dec_new_score(qrow_ref, kn_ref, eh)

    carry = lfn_ref[...] * LOG2E
    m = s_new
    for p in reversed(range(n_pages)):
        lf = lf_refs[p][...] * LOG2E
        cum = _lane_suffix(lf, tri)
        sc = s_sc[:, p * page:(p + 1) * page] + (cum + carry)
        s_sc[:, p * page:(p + 1) * page] = sc
        carry = carry + (cum[:, 0:1] + lf[:, 0:1])
        m = jnp.maximum(m, jnp.max(sc, axis=1, keepdims=True))

    lsum = jnp.zeros((N_HEADS, LANE), F32)
    for p in range(n_pages):
        pm = jnp.exp2(s_sc[:, p * page:(p + 1) * page] - m)
        lsum = lsum + pm
        w_sc[:, p * page:(p + 1) * page] = pm
    p_new = jnp.exp2(s_new - m)
    scale = 1.0 / (jnp.sum(lsum, axis=1, keepdims=True) + p_new)
    for p in range(n_pages):
        w_sc[:, p * page:(p + 1) * page] = w_sc[:, p * page:(p + 1) * page] * scale
    _dec_values(w_sc, vt_refs, acc_sc, page)
    _dec_output(acc_sc, p_new * scale, vn_ref, eh, o_ref)


def _decode_attention(layer, page_table, q, kn, vn, cache_k, cache_v, cache_lf=None, lfn=None):
    nb, d = q.shape
    n_pages = page_table.shape[1]
    page = cache_k.shape[2]
    assert page == LANE
    kt = jnp.transpose(cache_k, (0, 1, 3, 4, 2))
    vt = jnp.transpose(cache_v, (0, 1, 3, 4, 2))
    col_head = np.arange(d) // HEAD_DIM
    eh = jnp.asarray((np.arange(N_HEADS)[:, None] == col_head[None, :]).astype(np.float32), BF16)
    tri = jnp.asarray(np.tril(np.ones((page, page), np.float32), -1), BF16)
    qb = jnp.broadcast_to(q.reshape(nb, N_HEADS, HEAD_DIM, 1), (nb, N_HEADS, HEAD_DIM, LANE))
    row3 = lambda a: a.reshape(nb, 1, a.shape[1])
    row_spec = pl.BlockSpec((None, 1, d), lambda bi, pt: (bi, 0, 0))
    fixed = lambda shape: pl.BlockSpec(shape, lambda bi, pt: (0, 0))

    def page_spec(p):
        return pl.BlockSpec((None, None, N_HEADS, HEAD_DIM, page),
                            lambda bi, pt: (layer, pt[bi, p], 0, 0, 0))

    def lf_spec(p):
        return pl.BlockSpec((None, None, N_HEADS, page), lambda bi, pt: (layer, pt[bi, p], 0, 0))

    args = [qb, row3(q), row3(kn), row3(vn)]
    in_specs = [pl.BlockSpec((None, N_HEADS, HEAD_DIM, LANE), lambda bi, pt: (bi, 0, 0, 0)),
                row_spec, row_spec, row_spec]
    if cache_lf is not None:
        args.append(jnp.broadcast_to(lfn[:, :, None], (nb, N_HEADS, LANE)))
        in_specs.append(pl.BlockSpec((None, N_HEADS, LANE), lambda bi, pt: (bi, 0, 0)))
    args += [eh, tri]
    in_specs += [fixed((N_HEADS, d)), fixed((page, page))]
    args += [kt] * n_pages + [vt] * n_pages
    in_specs += [page_spec(p) for p in range(n_pages)] * 2
    if cache_lf is not None:
        args += [jnp.transpose(cache_lf, (0, 1, 3, 2))] * n_pages
        in_specs += [lf_spec(p) for p in range(n_pages)]
        kern = functools.partial(_dec_c_kernel, n_pages=n_pages, page=page)
    else:
        assert n_pages % (MOBA_BLOCK // page) == 0, "past length must be whole MoBA blocks"
        kern = functools.partial(_dec_ab_kernel, n_pages=n_pages, page=page)
    grid_spec = pltpu.PrefetchScalarGridSpec(
        num_scalar_prefetch=1, grid=(nb,), in_specs=in_specs,
        out_specs=pl.BlockSpec((None, 1, d), lambda bi, pt: (bi, 0, 0)),
        scratch_shapes=[pltpu.VMEM((N_HEADS, n_pages * page), F32),
                        pltpu.VMEM((N_HEADS, n_pages * page), F32),
                        pltpu.VMEM((N_HEADS, HEAD_DIM, LANE), F32)])
    out = pl.pallas_call(
        kern, grid_spec=grid_spec, out_shape=jax.ShapeDtypeStruct((nb, 1, d), BF16),
        compiler_params=_params(1, DEC_VMEM),
        name="dec_c" if cache_lf is not None else "dec_ab")(page_table, *args)
    return out.reshape(nb, d)


def _rope_tables(pos):
    half = ROT_DIM // 2
    inv = ROPE_THETA ** (-jnp.arange(half, dtype=F32) * 2.0 / ROT_DIM)
    ang = pos.astype(F32)[:, None] * inv[None, :]
    cos, sin = jnp.cos(ang), jnp.sin(ang)
    rest = jnp.zeros((pos.shape[0], HEAD_DIM - ROT_DIM), F32)
    zero = jnp.zeros_like(sin)
    c = jnp.concatenate([cos, cos, rest + 1.0], axis=1)
    s1 = jnp.concatenate([-sin, zero, rest], axis=1)
    s2 = jnp.concatenate([zero, sin, rest], axis=1)
    return tuple(jnp.concatenate([a, a], axis=1) for a in (c, s1, s2)), (cos.T, sin.T)


def kernel(x_prompt, x_sample, cache_k_ab, cache_v_ab, cache_k_c, cache_v_c, cache_logf_c, state_conv,
           page_table, w_in_ab, w_o_ab, w_in_c, b_f_c, w_o_c, g_mix, g_ffn, w_up, conv_w, conv_b,
           w_down, g_final):
    b, t, d = x_prompt.shape
    nb = x_sample.shape[0]
    assert x_sample.shape[1] == 1 and d == N_HEADS * HEAD_DIM
    depth = g_mix.shape[0]
    past_len = page_table.shape[1] * cache_k_ab.shape[2]
    m = b * t
    tm = min(512, t)
    tq = min(512, t)
    tf = min(512, t)
    cw = 256
    half = d // 2

    xp = x_prompt.reshape(m, d)
    xs = x_sample.reshape(nb, d)
    tabs_p, tabs_pt = _rope_tables(jnp.arange(t, dtype=jnp.int32))
    tabs_s, _ = _rope_tables(jnp.full((nb,), past_len, jnp.int32))
    n_ab, n_c = (depth + 1) // 2, depth // 2
    kv_t = {"k_ab_p": None, "v_ab_p": None, "k_c_p": None, "v_c_p": None}

    def t_out(key, w, layer, n_layers, rope):
        return dict(wt=w.T, tabs_t=tabs_pt if rope else None, layer=layer, n_layers=n_layers, batch=b,
                    prev=kv_t[key])

    outs = {k: [] for k in ("k_ab_p", "v_ab_p", "k_ab_s", "v_ab_s", "k_c_p", "v_c_p", "f_c_p",
                            "k_c_s", "v_c_s", "f_c_s", "conv_p", "conv_s")}
    for l in range(depth):
        i = l // 2
        gm = g_mix[l].reshape(1, d)
        if l % 2 == 0:
            w_in, w_o = w_in_ab[i].astype(BF16), w_o_ab[i].astype(BF16)
            wq, wk, wv = w_in[:, :d], w_in[:, d:2 * d], w_in[:, 2 * d:3 * d]
            (qb,) = _proj(xp, gm, wq, tm=tm, rope_tabs=tabs_p, emit_f32=False, out_scale=Q_SCALE)
            kf, kb, kv_t["k_ab_p"] = _proj(xp, gm, wk, tm=tm, rope_tabs=tabs_p,
                                           t_out=t_out("k_ab_p", wk, i, n_ab, True))
            vb, kv_t["v_ab_p"] = _proj(xp, gm, wv, tm=tm, emit_f32=False,
                                       t_out=t_out("v_ab_p", wv, i, n_ab, False))
            n_hp = H_SB * HEAD_DIM // LANE
            o_sb = _sb_attention(qb, kb, vb, b=b, t=t, tq=tq, n_hp=n_hp)
            km = _kmean(kf, col_block=1, width=half).reshape(b, t // MOBA_BLOCK, half)
            km = jnp.pad(km, ((0, 0), (0, LANE - t // MOBA_BLOCK), (0, 0)))
            o_mb = _moba_attention(qb, kb, vb, km, b=b, t=t, tq=tq, n_hp=n_hp, hp0=n_hp)
            xp = _oproj(xp, [o_sb, o_mb], [w_o[:half], w_o[half:]], tm=tm)
            (qs,) = _proj(xs, gm, wq, tm=nb, rope_tabs=tabs_s, emit_f32=False, out_scale=Q_SCALE)
            ksf, _ = _proj(xs, gm, wk, tm=nb, rope_tabs=tabs_s)
            vsf, _ = _proj(xs, gm, wv, tm=nb)
            o_s = _decode_attention(i, page_table, qs.astype(F32), ksf, vsf, cache_k_ab, cache_v_ab)
            xs = _oproj(xs, [o_s], [w_o], tm=nb)
            outs["k_ab_s"].append(ksf.reshape(nb, 1, N_HEADS, HEAD_DIM))
            outs["v_ab_s"].append(vsf.reshape(nb, 1, N_HEADS, HEAD_DIM))
        else:
            w_in, w_o = w_in_c[i].astype(BF16), w_o_c[i].astype(BF16)
            wq, wk, wv = w_in[:, :d], w_in[:, d:2 * d], w_in[:, 2 * d:3 * d]
            wf = jnp.pad(w_in[:, 3 * d:], ((0, 0), (0, LANE - N_HEADS)))
            bf = jnp.pad(b_f_c[i], (0, LANE - N_HEADS)).reshape(1, LANE)
            qb, lf = _proj(xp, gm, wq, tm=tm, gate=(wf, bf), emit_f32=False, out_scale=Q_SCALE)
            kb, kv_t["k_c_p"] = _proj(xp, gm, wk, tm=tm, emit_f32=False,
                                      t_out=t_out("k_c_p", wk, i, n_c, False))
            vb, kv_t["v_c_p"] = _proj(xp, gm, wv, tm=tm, emit_f32=False,
                                      t_out=t_out("v_c_p", wv, i, n_c, False))
            logf = lf[:, :N_HEADS].reshape(b, t, N_HEADS)
            pieces = _cumsum_time(jnp.swapaxes(logf, 1, 2), tc=tq)
            ka = jnp.stack(pieces, axis=-1).reshape(b, N_HEADS // 2, 2, t, 3)
            ka = ka.transpose(0, 1, 3, 2, 4).reshape(b, N_HEADS // 2, t, 6)
            ka = jnp.pad(ka, ((0, 0), (0, 0), (0, 0), (0, LANE - 6)))
            o_c = _fox_attention(qb, kb, vb, ka, b=b, t=t, tq=tq)
            xp = _oproj(xp, [o_c], [w_o], tm=tm)
            outs["f_c_p"].append(logf)
            qs, lfs = _proj(xs, gm, wq, tm=nb, gate=(wf, bf), emit_f32=False, out_scale=Q_SCALE)
            ksf, _ = _proj(xs, gm, wk, tm=nb)
            vsf, _ = _proj(xs, gm, wv, tm=nb)
            o_s = _decode_attention(i, page_table, qs.astype(F32), ksf, vsf, cache_k_c, cache_v_c,
                                    cache_logf_c, lfs[:, :N_HEADS])
            xs = _oproj(xs, [o_s], [w_o], tm=nb)
            outs["k_c_s"].append(ksf.reshape(nb, 1, N_HEADS, HEAD_DIM))
            outs["v_c_s"].append(vsf.reshape(nb, 1, N_HEADS, HEAD_DIM))
            outs["f_c_s"].append(lfs[:, :N_HEADS].reshape(nb, 1, N_HEADS))

        gf = g_ffn[l].reshape(1, d)
        wu, wd = w_up[l].astype(BF16), w_down[l].astype(BF16)
        cb = conv_b[l].reshape(1, -1)
        yp, conv_p = _ffn_prompt(xp.reshape(b, t, d), gf, wu, conv_w[l], cb, wd, tm=tf, cw=cw)
        xp = yp.reshape(m, d)
        xs, conv_s = _ffn_sample(xs, gf, wu, conv_w[l], cb, wd, state_conv[l], cw=cw)
        outs["conv_p"].append(conv_p)
        outs["conv_s"].append(conv_s)

    gfin = g_final.reshape(1, d)
    y_prompt = _final_norm(xp, gfin, tm=tm).reshape(b, t, d)
    y_sample = _final_norm(xs, gfin, tm=nb).reshape(nb, 1, d)
    st = lambda key: jnp.stack(outs[key])

    def kv(key):
        a = kv_t[key]
        return a.reshape(a.shape[0], b, N_HEADS, HEAD_DIM, t).transpose(0, 1, 4, 2, 3)

    return (y_prompt, y_sample, kv("k_ab_p"), kv("v_ab_p"), st("k_ab_s"), st("v_ab_s"),
            kv("k_c_p"), kv("v_c_p"), st("f_c_p"), st("k_c_s"), st("v_c_s"), st("f_c_s"),
            st("conv_p"), st("conv_s"))
```

```python
import functools

import numpy as np
import jax
import jax.numpy as jnp
from jax import lax
from jax.experimental import pallas as pl
from jax.experimental.pallas import tpu as pltpu

F32 = jnp.float32
BF16 = jnp.bfloat16

HEAD_DIM = 64
N_HEADS = 16
H_SB = N_HEADS // 2
ROT_DIM = HEAD_DIM // 4
ROPE_THETA = 500000.0
MOBA_BLOCK = 256
MOBA_TOPK = 3
RMS_EPS = 1e-6
ATTN_SCALE = HEAD_DIM ** -0.5
CONV_WIDTH = 3
LOG2E = 1.4426950408889634
Q_SCALE = ATTN_SCALE * LOG2E

LANE = 128
SUBLANE = 8
NEG = -1e30
EXIT_LOG2 = -160.0
PROJ_VMEM = 48 * 1024 * 1024
DEC_VMEM = 56 * 1024 * 1024

_NT = (((1,), (1,)), ((), ()))


def _params(n_axes, vmem=PROJ_VMEM):
    return pltpu.CompilerParams(dimension_semantics=("arbitrary",) * n_axes,
                                vmem_limit_bytes=vmem)


def _rms(x, g):
    ms = jnp.mean(x * x, axis=-1, keepdims=True)
    return x * lax.rsqrt(ms + RMS_EPS) * g


def _log_sigmoid(z):
    return jnp.minimum(z, 0.0) - jnp.log1p(jnp.exp(-jnp.abs(z)))


def _log2_sigmoid(z2):
    return jnp.minimum(z2, 0.0) - jnp.log2(1.0 + jnp.exp2(-jnp.abs(z2)))


def _split2(x):
    hi = x.astype(BF16)
    lo = (x - hi.astype(F32)).astype(BF16)
    return hi, lo


def _proj_kernel(*refs, rope, emit_f32, gate, out_scale, emit_t, aliased):
    it = iter(refs)
    x_ref, g_ref, w_ref = next(it), next(it), next(it)
    if rope:
        c_ref, s1_ref, s2_ref = next(it), next(it), next(it)
    if gate:
        wf_ref, bf_ref = next(it), next(it)
    if emit_t:
        wt_ref = next(it)
        if rope:
            ct_ref, st_ref = next(it), next(it)
        if aliased:
            next(it)
    if emit_f32:
        of_ref = next(it)
    ob_ref = next(it)
    if gate:
        og_ref = next(it)
    if emit_t:
        ot_ref = next(it)

    xn = _rms(x_ref[...], g_ref[...]).astype(BF16)
    y = jnp.dot(xn, w_ref[...], preferred_element_type=F32)
    if rope:
        half = y.shape[1] // 2
        c, s1, s2 = c_ref[...], s1_ref[...], s2_ref[...]
        parts = [y[:, :half]]
        for ch in range(half // LANE):
            yc = y[:, half + ch * LANE: half + (ch + 1) * LANE]
            up = pltpu.roll(yc, LANE - ROT_DIM // 2, 1)
            dn = pltpu.roll(yc, ROT_DIM // 2, 1)
            parts.append(yc * c + up * s1 + dn * s2)
        y = jnp.concatenate(parts, axis=1)
    if emit_f32:
        of_ref[...] = y
    ob_ref[...] = (y * out_scale if out_scale != 1.0 else y).astype(BF16)
    if gate:
        f = jnp.dot(xn, wf_ref[...], preferred_element_type=F32) + bf_ref[...]
        og_ref[...] = _log_sigmoid(f)
    if emit_t:
        yt = lax.dot_general(wt_ref[...], xn, _NT, preferred_element_type=F32)
        if rope:
            half = yt.shape[0] // 2
            rot = yt[half:].reshape(half // HEAD_DIM, HEAD_DIM, yt.shape[1])
            x1, x2 = rot[:, 0:ROT_DIM // 2], rot[:, ROT_DIM // 2:ROT_DIM]
            ct, st = ct_ref[...], st_ref[...]
            rot = jnp.concatenate([x1 * ct - x2 * st, x1 * st + x2 * ct, rot[:, ROT_DIM:]], axis=1)
            yt = jnp.concatenate([yt[:half], rot.reshape(half, yt.shape[1])], axis=0)
        ot_ref[...] = yt


def _proj(x, g, w, *, tm, rope_tabs=None, gate=None, emit_f32=True, out_scale=1.0, t_out=None):
    m, d = x.shape
    n = w.shape[1]
    row = lambda i: (i, 0)
    fixed = lambda i: (0, 0)
    in_specs = [pl.BlockSpec((tm, d), row), pl.BlockSpec((1, d), fixed), pl.BlockSpec((d, n), fixed)]
    args = [x, g, w]
    if rope_tabs is not None:
        n_tab_blocks = rope_tabs[0].shape[0] // tm
        tab = lambda i: (i % n_tab_blocks, 0)
        in_specs += [pl.BlockSpec((tm, LANE), tab)] * 3
        args += list(rope_tabs)
    if gate is not None:
        in_specs += [pl.BlockSpec((d, LANE), fixed), pl.BlockSpec((1, LANE), fixed)]
        args += list(gate)
    aliases = {}
    if t_out is not None:
        n_t = m // t_out["batch"] // tm
        in_specs.append(pl.BlockSpec((n, d), fixed))
        args.append(t_out["wt"])
        if rope_tabs is not None:
            in_specs += [pl.BlockSpec((ROT_DIM // 2, tm), lambda i: (0, i % n_t))] * 2
            args += list(t_out["tabs_t"])
        if t_out["prev"] is not None:
            aliases[len(args)] = int(emit_f32) + 1 + int(gate is not None)
            in_specs.append(pl.BlockSpec(memory_space=pl.ANY))
            args.append(t_out["prev"])
    out_shape, out_specs = [], []
    if emit_f32:
        out_shape.append(jax.ShapeDtypeStruct((m, n), F32))
        out_specs.append(pl.BlockSpec((tm, n), row))
    out_shape.append(jax.ShapeDtypeStruct((m, n), BF16))
    out_specs.append(pl.BlockSpec((tm, n), row))
    if gate is not None:
        out_shape.append(jax.ShapeDtypeStruct((m, LANE), F32))
        out_specs.append(pl.BlockSpec((tm, LANE), row))
    if t_out is not None:
        layer = t_out["layer"]
        out_shape.append(jax.ShapeDtypeStruct((t_out["n_layers"], t_out["batch"], n, m // t_out["batch"]), F32))
        out_specs.append(pl.BlockSpec((None, None, n, tm), lambda i: (layer, i // n_t, 0, i % n_t)))
    kern = functools.partial(_proj_kernel, rope=rope_tabs is not None, emit_f32=emit_f32,
                             gate=gate is not None, out_scale=out_scale, emit_t=t_out is not None,
                             aliased=bool(aliases))
    return pl.pallas_call(kern, grid=(m // tm,), in_specs=in_specs, out_specs=out_specs,
                          out_shape=out_shape, input_output_aliases=aliases,
                          compiler_params=_params(1), name="proj")(*args)


def _oproj_kernel(x_ref, *refs, n_parts):
    o_refs, w_refs, out_ref = refs[:n_parts], refs[n_parts:2 * n_parts], refs[2 * n_parts]
    y = jnp.dot(o_refs[0][...], w_refs[0][...], preferred_element_type=F32)
    for o_ref, w_ref in zip(o_refs[1:], w_refs[1:]):
        y = y + jnp.dot(o_ref[...], w_ref[...], preferred_element_type=F32)
    out_ref[...] = x_ref[...] + y


def _oproj(x, parts, weights, *, tm):
    m, d = x.shape
    row = lambda i: (i, 0)
    fixed = lambda i: (0, 0)
    in_specs = [pl.BlockSpec((tm, d), row)]
    in_specs += [pl.BlockSpec((tm, p.shape[1]), row) for p in parts]
    in_specs += [pl.BlockSpec(w.shape, fixed) for w in weights]
    return pl.pallas_call(
        functools.partial(_oproj_kernel, n_parts=len(parts)), grid=(m // tm,),
        in_specs=in_specs, out_specs=pl.BlockSpec((tm, d), row),
        out_shape=jax.ShapeDtypeStruct((m, d), F32), compiler_params=_params(1),
        name="oproj")(x, *parts, *weights)


def _silu_gate(gt, vl):
    return gt * (1.0 / (1.0 + jnp.exp(-gt))) * vl


def _ffn_prompt_kernel(x_ref, g_ref, wu_ref, cw_ref, cb_ref, wd_ref, out_ref, st_ref,
                       xn_sc, act_sc, ug_sc, uv_sc, carry_sc, *, cw):
    tm = x_ref.shape[0]
    dff = wd_ref.shape[0]

    @pl.when(pl.program_id(1) == 0)
    def _():
        carry_sc[...] = jnp.zeros(carry_sc.shape, F32)

    x = x_ref[...]
    xn_sc[...] = _rms(x, g_ref[...]).astype(BF16)

    def conv(col, u_sc):
        cols = slice(col, col + cw)
        u = jnp.dot(xn_sc[...], wu_ref[:, cols], preferred_element_type=F32)
        u_sc[0:SUBLANE, :] = carry_sc[:, cols]
        u_sc[SUBLANE:SUBLANE + tm, :] = u
        tail = u[tm - SUBLANE:tm, :]
        carry_sc[:, cols] = tail
        st_ref[:, cols] = tail
        w = cw_ref[:, cols]
        u0 = u_sc[SUBLANE - 2:SUBLANE - 2 + tm, :]
        u1 = u_sc[SUBLANE - 1:SUBLANE - 1 + tm, :]
        return ((cb_ref[:, cols] + w[0:1] * u0) + w[1:2] * u1) + w[2:3] * u

    for c in range(dff // cw):
        gt = conv(c * cw, ug_sc)
        vl = conv(dff + c * cw, uv_sc)
        act_sc[:, c * cw:(c + 1) * cw] = _silu_gate(gt, vl).astype(BF16)
    out_ref[...] = x + jnp.dot(act_sc[...], wd_ref[...], preferred_element_type=F32)


def _ffn_prompt(x, g, w_up, conv_w, conv_b, w_down, *, tm, cw):
    b, t, d = x.shape
    dff = w_down.shape[0]
    xmap = lambda bi, i: (bi, i, 0)
    resident = lambda shape: pl.BlockSpec(shape, lambda bi, i: (0, 0), pipeline_mode=pl.Buffered(1))
    in_specs = [pl.BlockSpec((None, tm, d), xmap), resident((1, d)), resident((d, 2 * dff)),
                resident((CONV_WIDTH, 2 * dff)), resident((1, 2 * dff)), resident((dff, d))]
    out_specs = [pl.BlockSpec((None, tm, d), xmap),
                 pl.BlockSpec((None, None, SUBLANE, 2 * dff), lambda bi, i: (bi, i, 0, 0))]
    out_shape = [jax.ShapeDtypeStruct((b, t, d), F32),
                 jax.ShapeDtypeStruct((b, t // tm, SUBLANE, 2 * dff), F32)]
    scratch = [pltpu.VMEM((tm, d), BF16), pltpu.VMEM((tm, dff), BF16),
               pltpu.VMEM((tm + SUBLANE, cw), F32), pltpu.VMEM((tm + SUBLANE, cw), F32),
               pltpu.VMEM((SUBLANE, 2 * dff), F32)]
    y, st = pl.pallas_call(
        functools.partial(_ffn_prompt_kernel, cw=cw), grid=(b, t // tm), in_specs=in_specs,
        out_specs=out_specs, out_shape=out_shape, scratch_shapes=scratch,
        compiler_params=_params(2, DEC_VMEM), name="ffn_prompt")(x, g, w_up, conv_w, conv_b, w_down)
    return y, st[:, -1, SUBLANE - (CONV_WIDTH - 1):, :]


def _ffn_sample_kernel(x_ref, g_ref, wg_ref, wv_ref, cwg_ref, cwv_ref, cbg_ref, cbv_ref, wd_ref,
                       s0g_ref, s0v_ref, s1g_ref, s1v_ref, out_ref, ug_ref, uv_ref, xn_sc, acc_sc):
    c = pl.program_id(0)

    @pl.when(c == 0)
    def _():
        x = x_ref[...]
        xn_sc[...] = _rms(x, g_ref[...]).astype(BF16)
        acc_sc[...] = x

    xn = xn_sc[...]

    def conv(w_ref, cw_ref, cb_ref, s0_ref, s1_ref, u_ref):
        u = jnp.dot(xn, w_ref[...], preferred_element_type=F32)
        u_ref[...] = u
        cw = cw_ref[...]
        return ((cb_ref[...] + cw[0:1] * s0_ref[...]) + cw[1:2] * s1_ref[...]) + cw[2:3] * u

    gt = conv(wg_ref, cwg_ref, cbg_ref, s0g_ref, s1g_ref, ug_ref)
    vl = conv(wv_ref, cwv_ref, cbv_ref, s0v_ref, s1v_ref, uv_ref)
    act = _silu_gate(gt, vl).astype(BF16)
    acc_sc[...] += jnp.dot(act, wd_ref[...], preferred_element_type=F32)

    @pl.when(c == pl.num_programs(0) - 1)
    def _():
        out_ref[...] = acc_sc[...]


def _ffn_sample(x, g, w_up, conv_w, conv_b, w_down, state, *, cw):
    m, d = x.shape
    dff = w_down.shape[0]
    nc = dff // cw
    st = state.reshape(m, (CONV_WIDTH - 1) * 2 * dff)
    fixed = lambda c: (0, 0)
    in_specs = [
        pl.BlockSpec((m, d), fixed),
        pl.BlockSpec((1, d), fixed),
        pl.BlockSpec((d, cw), lambda c: (0, c)),
        pl.BlockSpec((d, cw), lambda c: (0, c + nc)),
        pl.BlockSpec((CONV_WIDTH, cw), lambda c: (0, c)),
        pl.BlockSpec((CONV_WIDTH, cw), lambda c: (0, c + nc)),
        pl.BlockSpec((1, cw), lambda c: (0, c)),
        pl.BlockSpec((1, cw), lambda c: (0, c + nc)),
        pl.BlockSpec((cw, d), lambda c: (c, 0)),
        pl.BlockSpec((m, cw), lambda c: (0, c)),
        pl.BlockSpec((m, cw), lambda c: (0, c + nc)),
        pl.BlockSpec((m, cw), lambda c: (0, c + 2 * nc)),
        pl.BlockSpec((m, cw), lambda c: (0, c + 3 * nc)),
    ]
    out_specs = [pl.BlockSpec((m, d), fixed),
                 pl.BlockSpec((m, cw), lambda c: (0, c)),
                 pl.BlockSpec((m, cw), lambda c: (0, c))]
    out_shape = [jax.ShapeDtypeStruct((m, d), F32),
                 jax.ShapeDtypeStruct((m, dff), F32),
                 jax.ShapeDtypeStruct((m, dff), F32)]
    scratch = [pltpu.VMEM((m, d), BF16), pltpu.VMEM((m, d), F32)]
    y, ug, uv = pl.pallas_call(
        _ffn_sample_kernel, grid=(nc,), in_specs=in_specs, out_specs=out_specs,
        out_shape=out_shape, scratch_shapes=scratch, compiler_params=_params(1),
        name="ffn_sample")(x, g, w_up, w_up, conv_w, conv_w, conv_b, conv_b, w_down, st, st, st, st)
    u = jnp.concatenate([ug, uv], axis=-1)
    new_state = jnp.stack([state[:, 1, :], u], axis=1)
    return y, new_state


def _norm_kernel(x_ref, g_ref, o_ref):
    o_ref[...] = _rms(x_ref[...], g_ref[...])


def _final_norm(x, g, *, tm):
    m, d = x.shape
    return pl.pallas_call(
        _norm_kernel, grid=(m // tm,),
        in_specs=[pl.BlockSpec((tm, d), lambda i: (i, 0)), pl.BlockSpec((1, d), lambda i: (0, 0))],
        out_specs=pl.BlockSpec((tm, d), lambda i: (i, 0)),
        out_shape=jax.ShapeDtypeStruct((m, d), F32), compiler_params=_params(1),
        name="final_norm")(x, g)


def _split3(x):
    hi = x.astype(BF16)
    r1 = x - hi.astype(F32)
    mid = r1.astype(BF16)
    lo = (r1 - mid.astype(F32)).astype(BF16)
    return hi, mid, lo


def _cumsum_kernel(x_ref, u_ref, hi_ref, mid_ref, lo_ref, carry_sc):
    @pl.when(pl.program_id(1) == 0)
    def _():
        carry_sc[...] = jnp.zeros_like(carry_sc)

    u = u_ref[...]
    hi, mid, lo = _split3(x_ref[...])
    c = (jnp.dot(hi, u, preferred_element_type=F32) + jnp.dot(mid, u, preferred_element_type=F32)
         + jnp.dot(lo, u, preferred_element_type=F32)) + carry_sc[...]
    carry_sc[...] = c[:, c.shape[1] - 1:]
    hi_ref[...], mid_ref[...], lo_ref[...] = _split3(c * (-LOG2E))


def _cumsum_time(logf_t, *, tc):
    b, h, t = logf_t.shape
    upper = jnp.asarray(np.triu(np.ones((tc, tc), np.float32)), BF16)
    spec = pl.BlockSpec((None, h, tc), lambda bi, i: (bi, 0, i))
    return pl.pallas_call(
        _cumsum_kernel, grid=(b, t // tc),
        in_specs=[spec, pl.BlockSpec((tc, tc), lambda bi, i: (0, 0))],
        out_specs=[spec] * 3,
        out_shape=[jax.ShapeDtypeStruct((b, h, t), BF16)] * 3,
        scratch_shapes=[pltpu.VMEM((h, 1), F32)], compiler_params=_params(2),
        name="cumsum")(logf_t, upper)


def _kmean_kernel(k_ref, o_ref):
    o_ref[...] = jnp.mean(k_ref[...], axis=0, keepdims=True)


def _kmean(k_flat, *, col_block, width):
    m = k_flat.shape[0]
    nblk = m // MOBA_BLOCK
    out = pl.pallas_call(
        _kmean_kernel, grid=(nblk,),
        in_specs=[pl.BlockSpec((MOBA_BLOCK, width), lambda n: (n, col_block))],
        out_specs=pl.BlockSpec((None, 1, width), lambda n: (n, 0, 0)),
        out_shape=jax.ShapeDtypeStruct((nblk, 1, width), F32), compiler_params=_params(1),
        name="kmean")(k_flat)
    return out.reshape(nblk, width)


def _stack_heads(q2, tq):
    lane = lax.broadcasted_iota(jnp.int32, (tq, LANE), 1)
    zero = jnp.zeros_like(q2)
    return jnp.where(lane < HEAD_DIM, q2, zero), jnp.where(lane >= HEAD_DIM, q2, zero)


def _unstack_heads(o, tq):
    lane = lax.broadcasted_iota(jnp.int32, (tq, LANE), 1)
    return jnp.where(lane < HEAD_DIM, o[:tq], o[tq:])


def _causal_keep(tq, tk, strict):
    r = lax.broadcasted_iota(jnp.int32, (2 * tq, tk), 0)
    r = jnp.where(r >= tq, r - tq, r)
    c = lax.broadcasted_iota(jnp.int32, (2 * tq, tk), 1)
    return (c < r) if strict else (c <= r)


def _softmax_step(s, v2, m_sc, acc_sc):
    tq = s.shape[0] // 2
    m_prev = m_sc[...]
    m_new = jnp.maximum(m_prev, jnp.max(s, axis=1, keepdims=True))
    alpha = jnp.exp2(m_prev - m_new)
    p = jnp.exp2(s - jnp.concatenate([m_new] * (s.shape[1] // LANE), axis=1)).astype(BF16)
    lane = lax.broadcasted_iota(jnp.int32, v2.shape, 1)
    one = jnp.ones_like(v2)
    pv = jnp.concatenate(
        [jnp.dot(p[:tq], jnp.where(lane < HEAD_DIM, v2, one), preferred_element_type=F32),
         jnp.dot(p[tq:], jnp.where(lane >= HEAD_DIM, v2, one), preferred_element_type=F32)], axis=0)
    acc_sc[...] = alpha * acc_sc[...] + pv
    m_sc[...] = m_new


def _softmax_finish(acc_sc, tq):
    acc = acc_sc[...]
    o = acc / pltpu.roll(acc, HEAD_DIM, 1)
    return _unstack_heads(o, tq).astype(BF16)


def _flash_loop(qi, tq, lhs_sc, k_ref, aug_ref, v_ref, m_sc, acc_sc, negligible=None):
    m_sc[...] = jnp.full(m_sc.shape, NEG, F32)
    acc_sc[...] = jnp.zeros(acc_sc.shape, F32)

    def chunk(j, diag):
        off = pl.multiple_of(j * tq, tq)
        kaug = jnp.concatenate([k_ref[pl.ds(off, tq), :], aug_ref[pl.ds(off, tq), :]], axis=1)
        s = lax.dot_general(lhs_sc[...], kaug, _NT, preferred_element_type=F32)
        if diag:
            s = jnp.where(_causal_keep(tq, tq, False), s, NEG)
        _softmax_step(s, v_ref[pl.ds(off, tq), :], m_sc, acc_sc)

    chunk(qi, True)

    @pl.when(qi > 0)
    def _():
        chunk(qi - 1, False)

    def cond(jj):
        more = jj < qi
        if negligible is None:
            return more
        return jnp.logical_and(more, jnp.logical_not(negligible(jnp.maximum(qi - 1 - jj, 0))))

    def body(jj):
        chunk(qi - 1 - jj, False)
        return jj + 1

    lax.while_loop(cond, body, jnp.int32(1))


def _fox_kernel(q_ref, k_ref, v_ref, ka_ref, o_ref, lhs_sc, m_sc, acc_sc, ub_sc, kmax_sc, bmax_sc,
                *, tq):
    qi = pl.program_id(2)
    n_pieces = 3
    qa, qb = _stack_heads(q_ref[...], tq)
    lhs_sc[0:tq, 0:LANE] = qa
    lhs_sc[tq:2 * tq, 0:LANE] = qb
    lane = lax.broadcasted_iota(jnp.int32, (2 * tq, LANE), 1)
    row = lax.broadcasted_iota(jnp.int32, (2 * tq, LANE), 0)
    piece_head = jnp.where(lane < n_pieces, 0, jnp.where(lane < 2 * n_pieces, 1, 2))
    lhs_sc[:, LANE:2 * LANE] = jnp.where(piece_head == jnp.where(row >= tq, 1, 0), 1.0, 0.0).astype(BF16)

    @pl.when(qi == 0)
    def _():
        def kbody(j, acc):
            blk = k_ref[pl.ds(pl.multiple_of(j * tq, tq), tq), :].astype(F32)
            return jnp.maximum(acc, jnp.max(jnp.abs(blk), axis=0, keepdims=True))

        n_chunks = k_ref.shape[0] // tq
        kabs = lax.fori_loop(0, n_chunks, kbody, jnp.zeros((1, LANE), F32))
        lane1 = lax.broadcasted_iota(jnp.int32, (1, LANE), 1)
        kmax_sc[0] = jnp.max(jnp.where(lane1 < HEAD_DIM, kabs, 0.0))
        kmax_sc[1] = jnp.max(jnp.where(lane1 >= HEAD_DIM, kabs, 0.0))

        def bbody(j, carry):
            off = pl.multiple_of(j * tq + tq - 2 * SUBLANE, 2 * SUBLANE)
            tail = ka_ref[pl.ds(off, 2 * SUBLANE), :].astype(F32)
            r16 = lax.broadcasted_iota(jnp.int32, tail.shape, 0)
            l16 = lax.broadcasted_iota(jnp.int32, tail.shape, 1)
            last = jnp.where(r16 == 2 * SUBLANE - 1, tail, 0.0)
            bmax_sc[2 * j] = jnp.sum(jnp.where(l16 < n_pieces, last, 0.0))
            bmax_sc[2 * j + 1] = jnp.sum(
                jnp.where(l16 < n_pieces, 0.0, jnp.where(l16 < 2 * n_pieces, last, 0.0)))
            return carry

        lax.fori_loop(0, n_chunks, bbody, 0)

    q1 = jnp.sum(jnp.abs(lhs_sc[:, 0:LANE].astype(F32)), axis=1, keepdims=True)
    ub_sc[...] = jnp.broadcast_to(q1, (2 * tq, LANE)) * jnp.where(row >= tq, kmax_sc[1], kmax_sc[0])

    def negligible(j):
        gap = ub_sc[...] + jnp.where(row >= tq, bmax_sc[2 * j + 1], bmax_sc[2 * j]) - m_sc[...]
        return jnp.max(gap) < EXIT_LOG2

    _flash_loop(qi, tq, lhs_sc, k_ref, ka_ref, v_ref, m_sc, acc_sc, negligible)
    o_ref[...] = _softmax_finish(acc_sc, tq)


def _flash_scratch(tq):
    return [pltpu.VMEM((2 * tq, 2 * LANE), BF16), pltpu.VMEM((2 * tq, LANE), F32),
            pltpu.VMEM((2 * tq, LANE), F32)]


def _fox_attention(q, k, v, ka, *, b, t, tq):
    m = q.shape[0]
    n_hp = q.shape[1] // LANE
    nq = t // tq
    return pl.pallas_call(
        functools.partial(_fox_kernel, tq=tq), grid=(b, n_hp, nq),
        in_specs=[pl.BlockSpec((tq, LANE), lambda bi, hp, qi: (bi * nq + qi, hp)),
                  pl.BlockSpec((t, LANE), lambda bi, hp, qi: (bi, hp)),
                  pl.BlockSpec((t, LANE), lambda bi, hp, qi: (bi, hp)),
                  pl.BlockSpec((None, None, t, LANE), lambda bi, hp, qi: (bi, hp, 0, 0))],
        out_specs=pl.BlockSpec((tq, LANE), lambda bi, hp, qi: (bi * nq + qi, hp)),
        out_shape=jax.ShapeDtypeStruct((m, n_hp * LANE), BF16),
        scratch_shapes=_flash_scratch(tq) + [pltpu.VMEM((2 * tq, LANE), F32), pltpu.SMEM((2,), F32),
                                             pltpu.SMEM((2 * nq,), F32)],
        compiler_params=_params(3), name="fox_attn")(q, k, v, ka)


def _sb_kernel(q_ref, k_ref, v_ref, u_ref, o_ref, qs_sc, r_sc, acc_sc, *, tq, sub):
    qi = pl.program_id(2)
    qa, qb = _stack_heads(q_ref[...], tq)
    qs_sc[0:tq, :] = qa
    qs_sc[tq:2 * tq, :] = qb
    r_sc[...] = jnp.zeros(r_sc.shape, F32)
    acc_sc[...] = jnp.zeros(acc_sc.shape, F32)
    u = u_ref[...]

    def chunk(j, diag):
        off = pl.multiple_of(j * tq, tq)
        v2 = v_ref[pl.ds(off, tq), :]
        z_all = lax.dot_general(qs_sc[...], k_ref[pl.ds(off, tq), :], _NT, preferred_element_type=F32)
        keep_all = _causal_keep(tq, tq, True) if diag else None
        for sb in reversed(range(tq // sub)):
            z = z_all[:, sb * sub:(sb + 1) * sub]
            neg_abs = lax.bitcast_convert_type(
                lax.bitcast_convert_type(z, jnp.uint32) | jnp.uint32(0x80000000), F32)
            sp = jnp.maximum(z, 0.0) + jnp.log2(1.0 + jnp.exp2(neg_abs))
            if diag:
                keep = keep_all[:, sb * sub:(sb + 1) * sub]
                sp = jnp.where(keep, sp, 0.0)
            cum = jnp.dot(sp.astype(BF16), u, preferred_element_type=F32)
            r = r_sc[...]
            w = jnp.exp2((z - sp) + (cum + jnp.concatenate([r] * (sub // LANE), axis=1)))
            if diag:
                w = jnp.where(keep, w, 0.0)
            acc_sc[...] += jnp.dot(w.astype(BF16), v2[sb * sub:(sb + 1) * sub, :],
                                   preferred_element_type=F32)
            r_sc[...] = r + (cum[:, 0:1] - sp[:, 0:1])

    chunk(qi, True)

    @pl.when(qi > 0)
    def _():
        chunk(qi - 1, False)

    def cond(jj):
        return jnp.logical_and(jj < qi, jnp.max(r_sc[...]) >= EXIT_LOG2)

    def body(jj):
        chunk(qi - 1 - jj, False)
        return jj + 1

    lax.while_loop(cond, body, jnp.int32(1))
    o_ref[...] = _unstack_heads(acc_sc[...], tq).astype(BF16)


def _sb_attention(q, k, v, *, b, t, tq, n_hp, sub=256):
    m = q.shape[0]
    nq = t // tq
    lower = jnp.asarray(-np.tril(np.ones((sub, sub), np.float32), -1), BF16)
    scratch = [pltpu.VMEM((2 * tq, LANE), BF16), pltpu.VMEM((2 * tq, LANE), F32),
               pltpu.VMEM((2 * tq, LANE), F32)]
    return pl.pallas_call(
        functools.partial(_sb_kernel, tq=tq, sub=sub), grid=(b, n_hp, nq),
        in_specs=[pl.BlockSpec((tq, LANE), lambda bi, hp, qi: (bi * nq + qi, hp)),
                  pl.BlockSpec((t, LANE), lambda bi, hp, qi: (bi, hp)),
                  pl.BlockSpec((t, LANE), lambda bi, hp, qi: (bi, hp)),
                  pl.BlockSpec((sub, sub), lambda bi, hp, qi: (0, 0))],
        out_specs=pl.BlockSpec((tq, LANE), lambda bi, hp, qi: (bi * nq + qi, hp)),
        out_shape=jax.ShapeDtypeStruct((m, n_hp * LANE), BF16),
        scratch_shapes=scratch, compiler_params=_params(3), name="sb_attn")(q, k, v, lower)


def _moba_kernel(q_ref, k_ref, v_ref, km_ref, et_ref, o_ref, lhs_sc, m_sc, acc_sc, *, tq):
    qi = pl.program_id(2)
    qa, qb = _stack_heads(q_ref[...], tq)
    lhs_sc[0:tq, 0:LANE] = qa
    lhs_sc[tq:2 * tq, 0:LANE] = qb

    qs = lhs_sc[:, 0:LANE]
    km_hi, km_lo = _split2(km_ref[...])
    gate = (lax.dot_general(qs, km_hi, _NT, preferred_element_type=F32)
            + lax.dot_general(qs, km_lo, _NT, preferred_element_type=F32))
    n_idx = lax.broadcasted_iota(jnp.int32, (2 * tq, LANE), 1)
    row = lax.broadcasted_iota(jnp.int32, (2 * tq, LANE), 0)
    pos = qi * tq + jnp.where(row >= tq, row - tq, row)
    q_blk = jnp.right_shift(pos, MOBA_BLOCK.bit_length() - 1)
    n_f = n_idx.astype(F32)
    g = jnp.where(n_idx < q_blk, gate, -jnp.inf)
    bias = jnp.where(n_idx == q_blk, 0.0, NEG)
    for _ in range(MOBA_TOPK):
        mx = jnp.max(g, axis=1, keepdims=True)
        first = jnp.min(jnp.where(g == mx, n_f, float(LANE)), axis=1, keepdims=True)
        hit = n_f == first
        bias = jnp.where(jnp.where(hit, n_idx, q_blk) < q_blk, 0.0, bias)
        g = jnp.where(hit, -jnp.inf, g)
    lhs_sc[:, LANE:2 * LANE] = bias.astype(BF16)
    _flash_loop(qi, tq, lhs_sc, k_ref, et_ref, v_ref, m_sc, acc_sc)
    o_ref[...] = _softmax_finish(acc_sc, tq)


def _moba_attention(q, k, v, kmean_pad, *, b, t, tq, n_hp, hp0):
    m = q.shape[0]
    nq = t // tq
    et = (np.arange(t)[:, None] // MOBA_BLOCK == np.arange(LANE)[None, :]).astype(np.float32)
    et = jnp.asarray(et, BF16)
    scratch = _flash_scratch(tq)
    return pl.pallas_call(
        functools.partial(_moba_kernel, tq=tq), grid=(b, n_hp, nq),
        in_specs=[pl.BlockSpec((tq, LANE), lambda bi, hp, qi: (bi * nq + qi, hp0 + hp)),
                  pl.BlockSpec((t, LANE), lambda bi, hp, qi: (bi, hp0 + hp)),
                  pl.BlockSpec((t, LANE), lambda bi, hp, qi: (bi, hp0 + hp)),
                  pl.BlockSpec((None, LANE, LANE), lambda bi, hp, qi: (bi, 0, hp)),
                  pl.BlockSpec((t, LANE), lambda bi, hp, qi: (0, 0))],
        out_specs=pl.BlockSpec((tq, LANE), lambda bi, hp, qi: (bi * nq + qi, hp)),
        out_shape=jax.ShapeDtypeStruct((m, n_hp * LANE), BF16),
        scratch_shapes=scratch, compiler_params=_params(3), name="moba_attn")(q, k, v, kmean_pad, et)


def _dec_scores(qb_ref, kt_refs, s_sc, page):
    for h in range(N_HEADS):
        qh = qb_ref[h]
        for p, kt_ref in enumerate(kt_refs):
            s_sc[h:h + 1, p * page:(p + 1) * page] = jnp.sum(kt_ref[h] * qh, axis=0, keepdims=True)


def _dec_new_score(qrow_ref, kn_ref, eh):
    prod = jnp.broadcast_to(kn_ref[...] * qrow_ref[...], (LANE, eh.shape[1]))
    hi, lo = _split2(prod)
    return (lax.dot_general(eh, hi, _NT, preferred_element_type=F32)
            + lax.dot_general(eh, lo, _NT, preferred_element_type=F32))


def _lane_suffix(x, tri):
    hi, lo = _split2(x)
    return jnp.dot(hi, tri, preferred_element_type=F32) + jnp.dot(lo, tri, preferred_element_type=F32)


def _dec_values(w_sc, vt_refs, acc_sc, page):
    for h in range(N_HEADS):
        acc = jnp.zeros(acc_sc.shape[1:], F32)
        for p, vt_ref in enumerate(vt_refs):
            acc = acc + vt_ref[h] * w_sc[h:h + 1, p * page:(p + 1) * page]
        acc_sc[h] = acc


def _dec_output(acc_sc, p_new, vn_ref, eh, o_ref):
    d = eh.shape[1]
    hi, lo = _split2(acc_sc[...].reshape(d, LANE))
    ones = jnp.ones((SUBLANE, LANE), BF16)
    o = (lax.dot_general(ones, hi, _NT, preferred_element_type=F32)
         + lax.dot_general(ones, lo, _NT, preferred_element_type=F32))[0:1]
    pn = jnp.sum(jnp.concatenate([p_new] * (d // LANE), axis=1) * eh.astype(F32), axis=0, keepdims=True)
    o_ref[...] = (o + pn * vn_ref[...]).astype(BF16)


def _dec_ab_kernel(pt_ref, qb_ref, qrow_ref, kn_ref, vn_ref, eh_ref, tri_ref, *rest, n_pages, page):
    kt_refs, vt_refs = rest[:n_pages], rest[n_pages:2 * n_pages]
    o_ref, s_sc, w_sc, acc_sc = rest[2 * n_pages:2 * n_pages + 4]
    eh, tri = eh_ref[...], tri_ref[...]
    lane_f = lax.broadcasted_iota(jnp.int32, (N_HEADS, LANE), 1).astype(F32)
    is_sb = lax.broadcasted_iota(jnp.int32, (N_HEADS, LANE), 0) < H_SB

    _dec_scores(qb_ref, kt_refs, s_sc, page)
    s_new = _dec_new_score(qrow_ref, kn_ref, eh)

    ppb = MOBA_BLOCK // page
    nblk = n_pages // ppb
    g = jnp.full((N_HEADS, LANE), -jnp.inf, F32)
    for n in range(nblk):
        gn = jnp.sum(s_sc[:, n * MOBA_BLOCK:(n + 1) * MOBA_BLOCK], axis=1, keepdims=True)
        g = jnp.where(lane_f == float(n), gn, g)
    firsts = []
    for _ in range(min(MOBA_TOPK, nblk + 1)):
        mx = jnp.max(g, axis=1, keepdims=True)
        first = jnp.min(jnp.where(g == mx, lane_f, float(LANE)), axis=1, keepdims=True)
        firsts.append(first)
        g = jnp.where(lane_f == first, -jnp.inf, g)
    sel_bias = []
    for n in range(nblk):
        miss = functools.reduce(jnp.minimum, [jnp.abs(f - float(n)) for f in firsts])
        sel_bias.append(jnp.where(miss == 0.0, 0.0, NEG))
    m = s_new
    for p in range(n_pages):
        sc = s_sc[:, p * page:(p + 1) * page] + sel_bias[p // ppb]
        m = jnp.maximum(m, jnp.max(sc, axis=1, keepdims=True))

    carry = jnp.zeros((N_HEADS, LANE), F32)
    lsum = jnp.zeros((N_HEADS, LANE), F32)
    for p in reversed(range(n_pages)):
        z = s_sc[:, p * page:(p + 1) * page]
        ls = _log2_sigmoid(z)
        l1m = ls - z
        cum = _lane_suffix(l1m, tri)
        w_sb = jnp.exp2(ls + (cum + carry))
        carry = carry + (cum[:, 0:1] + l1m[:, 0:1])
        pm = jnp.exp2((z + sel_bias[p // ppb]) - m)
        lsum = lsum + pm
        w_sc[:, p * page:(p + 1) * page] = jnp.where(is_sb, w_sb, pm)
    p_new = jnp.where(is_sb, 0.0, jnp.exp2(s_new - m))
    scale = jnp.where(is_sb, 1.0, 1.0 / (jnp.sum(lsum, axis=1, keepdims=True) + p_new))
    for p in range(n_pages):
        w_sc[:, p * page:(p + 1) * page] = w_sc[:, p * page:(p + 1) * page] * scale
    _dec_values(w_sc, vt_refs, acc_sc, page)
    _dec_output(acc_sc, p_new * scale, vn_ref, eh, o_ref)


def _dec_c_kernel(pt_ref, qb_ref, qrow_ref, kn_ref, vn_ref, lfn_ref, eh_ref, tri_ref, *rest,
                  n_pages, page):
    kt_refs, vt_refs = rest[:n_pages], rest[n_pages:2 * n_pages]
    lf_refs = rest[2 * n_pages:3 * n_pages]
    o_ref, s_sc, w_sc, acc_sc = rest[3 * n_pages:3 * n_pages + 4]
    eh, tri = eh_ref[...], tri_ref[...]

    _dec_scores(qb_ref, kt_refs, s_sc, page)
    s_new = _dec_new_score(qrow_ref, kn_ref, eh)

    carry = lfn_ref[...] * LOG2E
    m = s_new
    for p in reversed(range(n_pages)):
        lf = lf_refs[p][...] * LOG2E
        cum = _lane_suffix(lf, tri)
        sc = s_sc[:, p * page:(p + 1) * page] + (cum + carry)
        s_sc[:, p * page:(p + 1) * page] = sc
        carry = carry + (cum[:, 0:1] + lf[:, 0:1])
        m = jnp.maximum(m, jnp.max(sc, axis=1, keepdims=True))

    lsum = jnp.zeros((N_HEADS, LANE), F32)
    for p in range(n_pages):
        pm = jnp.exp2(s_sc[:, p * page:(p + 1) * page] - m)
        lsum = lsum + pm
        w_sc[:, p * page:(p + 1) * page] = pm
    p_new = jnp.exp2(s_new - m)
    scale = 1.0 / (jnp.sum(lsum, axis=1, keepdims=True) + p_new)
    for p in range(n_pages):
        w_sc[:, p * page:(p + 1) * page] = w_sc[:, p * page:(p + 1) * page] * scale
    _dec_values(w_sc, vt_refs, acc_sc, page)
    _dec_output(acc_sc, p_new * scale, vn_ref, eh, o_ref)


def _decode_attention(layer, page_table, q, kn, vn, cache_k, cache_v, cache_lf=None, lfn=None):
    nb, d = q.shape
    n_pages = page_table.shape[1]
    page = cache_k.shape[2]
    assert page == LANE
    kt = jnp.transpose(cache_k, (0, 1, 3, 4, 2))
    vt = jnp.transpose(cache_v, (0, 1, 3, 4, 2))
    col_head = np.arange(d) // HEAD_DIM
    eh = jnp.asarray((np.arange(N_HEADS)[:, None] == col_head[None, :]).astype(np.float32), BF16)
    tri = jnp.asarray(np.tril(np.ones((page, page), np.float32), -1), BF16)
    qb = jnp.broadcast_to(q.reshape(nb, N_HEADS, HEAD_DIM, 1), (nb, N_HEADS, HEAD_DIM, LANE))
    row3 = lambda a: a.reshape(nb, 1, a.shape[1])
    row_spec = pl.BlockSpec((None, 1, d), lambda bi, pt: (bi, 0, 0))
    fixed = lambda shape: pl.BlockSpec(shape, lambda bi, pt: (0, 0))

    def page_spec(p):
        return pl.BlockSpec((None, None, N_HEADS, HEAD_DIM, page),
                            lambda bi, pt: (layer, pt[bi, p], 0, 0, 0))

    def lf_spec(p):
        return pl.BlockSpec((None, None, N_HEADS, page), lambda bi, pt: (layer, pt[bi, p], 0, 0))

    args = [qb, row3(q), row3(kn), row3(vn)]
    in_specs = [pl.BlockSpec((None, N_HEADS, HEAD_DIM, LANE), lambda bi, pt: (bi, 0, 0, 0)),
                row_spec, row_spec, row_spec]
    if cache_lf is not None:
        args.append(jnp.broadcast_to(lfn[:, :, None], (nb, N_HEADS, LANE)))
        in_specs.append(pl.BlockSpec((None, N_HEADS, LANE), lambda bi, pt: (bi, 0, 0)))
    args += [eh, tri]
    in_specs += [fixed((N_HEADS, d)), fixed((page, page))]
    args += [kt] * n_pages + [vt] * n_pages
    in_specs += [page_spec(p) for p in range(n_pages)] * 2
    if cache_lf is not None:
        args += [jnp.transpose(cache_lf, (0, 1, 3, 2))] * n_pages
        in_specs += [lf_spec(p) for p in range(n_pages)]
        kern = functools.partial(_dec_c_kernel, n_pages=n_pages, page=page)
    else:
        assert n_pages % (MOBA_BLOCK // page) == 0, "past length must be whole MoBA blocks"
        kern = functools.partial(_dec_ab_kernel, n_pages=n_pages, page=page)
    grid_spec = pltpu.PrefetchScalarGridSpec(
        num_scalar_prefetch=1, grid=(nb,), in_specs=in_specs,
        out_specs=pl.BlockSpec((None, 1, d), lambda bi, pt: (bi, 0, 0)),
        scratch_shapes=[pltpu.VMEM((N_HEADS, n_pages * page), F32),
                        pltpu.VMEM((N_HEADS, n_pages * page), F32),
                        pltpu.VMEM((N_HEADS, HEAD_DIM, LANE), F32)])
    out = pl.pallas_call(
        kern, grid_spec=grid_spec, out_shape=jax.ShapeDtypeStruct((nb, 1, d), BF16),
        compiler_params=_params(1, DEC_VMEM),
        name="dec_c" if cache_lf is not None else "dec_ab")(page_table, *args)
    return out.reshape(nb, d)


def _rope_tables(pos):
    half = ROT_DIM // 2
    inv = ROPE_THETA ** (-jnp.arange(half, dtype=F32) * 2.0 / ROT_DIM)
    ang = pos.astype(F32)[:, None] * inv[None, :]
    cos, sin = jnp.cos(ang), jnp.sin(ang)
    rest = jnp.zeros((pos.shape[0], HEAD_DIM - ROT_DIM), F32)
    zero = jnp.zeros_like(sin)
    c = jnp.concatenate([cos, cos, rest + 1.0], axis=1)
    s1 = jnp.concatenate([-sin, zero, rest], axis=1)
    s2 = jnp.concatenate([zero, sin, rest], axis=1)
    return tuple(jnp.concatenate([a, a], axis=1) for a in (c, s1, s2)), (cos.T, sin.T)


def kernel(x_prompt, x_sample, cache_k_ab, cache_v_ab, cache_k_c, cache_v_c, cache_logf_c, state_conv,
           page_table, w_in_ab, w_o_ab, w_in_c, b_f_c, w_o_c, g_mix, g_ffn, w_up, conv_w, conv_b,
           w_down, g_final):
    b, t, d = x_prompt.shape
    nb = x_sample.shape[0]
    assert x_sample.shape[1] == 1 and d == N_HEADS * HEAD_DIM
    depth = g_mix.shape[0]
    past_len = page_table.shape[1] * cache_k_ab.shape[2]
    m = b * t
    tm = min(512, t)
    tq = min(512, t)
    tf = min(256, t)
    cw = 256
    half = d // 2

    xp = x_prompt.reshape(m, d)
    xs = x_sample.reshape(nb, d)
    tabs_p, tabs_pt = _rope_tables(jnp.arange(t, dtype=jnp.int32))
    tabs_s, _ = _rope_tables(jnp.full((nb,), past_len, jnp.int32))
    n_ab, n_c = (depth + 1) // 2, depth // 2
    kv_t = {"k_ab_p": None, "v_ab_p": None, "k_c_p": None, "v_c_p": None}

    def t_out(key, w, layer, n_layers, rope):
        return dict(wt=w.T, tabs_t=tabs_pt if rope else None, layer=layer, n_layers=n_layers, batch=b,
                    prev=kv_t[key])

    outs = {k: [] for k in ("k_ab_p", "v_ab_p", "k_ab_s", "v_ab_s", "k_c_p", "v_c_p", "f_c_p",
                            "k_c_s", "v_c_s", "f_c_s", "conv_p", "conv_s")}
    for l in range(depth):
        i = l // 2
        gm = g_mix[l].reshape(1, d)
        if l % 2 == 0:
            w_in, w_o = w_in_ab[i].astype(BF16), w_o_ab[i].astype(BF16)
            wq, wk, wv = w_in[:, :d], w_in[:, d:2 * d], w_in[:, 2 * d:3 * d]
            (qb,) = _proj(xp, gm, wq, tm=tm, rope_tabs=tabs_p, emit_f32=False, out_scale=Q_SCALE)
            kf, kb, kv_t["k_ab_p"] = _proj(xp, gm, wk, tm=tm, rope_tabs=tabs_p,
                                           t_out=t_out("k_ab_p", wk, i, n_ab, True))
            vb, kv_t["v_ab_p"] = _proj(xp, gm, wv, tm=tm, emit_f32=False,
                                       t_out=t_out("v_ab_p", wv, i, n_ab, False))
            n_hp = H_SB * HEAD_DIM // LANE
            o_sb = _sb_attention(qb, kb, vb, b=b, t=t, tq=tq, n_hp=n_hp)
            km = _kmean(kf, col_block=1, width=half).reshape(b, t // MOBA_BLOCK, half)
            km = jnp.pad(km, ((0, 0), (0, LANE - t // MOBA_BLOCK), (0, 0)))
            o_mb = _moba_attention(qb, kb, vb, km, b=b, t=t, tq=tq, n_hp=n_hp, hp0=n_hp)
            xp = _oproj(xp, [o_sb, o_mb], [w_o[:half], w_o[half:]], tm=tm)
            (qs,) = _proj(xs, gm, wq, tm=nb, rope_tabs=tabs_s, emit_f32=False, out_scale=Q_SCALE)
            ksf, _ = _proj(xs, gm, wk, tm=nb, rope_tabs=tabs_s)
            vsf, _ = _proj(xs, gm, wv, tm=nb)
            o_s = _decode_attention(i, page_table, qs.astype(F32), ksf, vsf, cache_k_ab, cache_v_ab)
            xs = _oproj(xs, [o_s], [w_o], tm=nb)
            outs["k_ab_s"].append(ksf.reshape(nb, 1, N_HEADS, HEAD_DIM))
            outs["v_ab_s"].append(vsf.reshape(nb, 1, N_HEADS, HEAD_DIM))
        else:
            w_in, w_o = w_in_c[i].astype(BF16), w_o_c[i].astype(BF16)
            wq, wk, wv = w_in[:, :d], w_in[:, d:2 * d], w_in[:, 2 * d:3 * d]
            wf = jnp.pad(w_in[:, 3 * d:], ((0, 0), (0, LANE - N_HEADS)))
            bf = jnp.pad(b_f_c[i], (0, LANE - N_HEADS)).reshape(1, LANE)
            qb, lf = _proj(xp, gm, wq, tm=tm, gate=(wf, bf), emit_f32=False, out_scale=Q_SCALE)
            kb, kv_t["k_c_p"] = _proj(xp, gm, wk, tm=tm, emit_f32=False,
                                      t_out=t_out("k_c_p", wk, i, n_c, False))
            vb, kv_t["v_c_p"] = _proj(xp, gm, wv, tm=tm, emit_f32=False,
                                      t_out=t_out("v_c_p", wv, i, n_c, False))
            logf = lf[:, :N_HEADS].reshape(b, t, N_HEADS)
            pieces = _cumsum_time(jnp.swapaxes(logf, 1, 2), tc=tq)
            ka = jnp.stack(pieces, axis=-1).reshape(b, N_HEADS // 2, 2, t, 3)
            ka = ka.transpose(0, 1, 3, 2, 4).reshape(b, N_HEADS // 2, t, 6)
            ka = jnp.pad(ka, ((0, 0), (0, 0), (0, 0), (0, LANE - 6)))
            o_c = _fox_attention(qb, kb, vb, ka, b=b, t=t, tq=tq)
            xp = _oproj(xp, [o_c], [w_o], tm=tm)
            outs["f_c_p"].append(logf)
            qs, lfs = _proj(xs, gm, wq, tm=nb, gate=(wf, bf), emit_f32=False, out_scale=Q_SCALE)
            ksf, _ = _proj(xs, gm, wk, tm=nb)
            vsf, _ = _proj(xs, gm, wv, tm=nb)
            o_s = _decode_attention(i, page_table, qs.astype(F32), ksf, vsf, cache_k_c, cache_v_c,
                                    cache_logf_c, lfs[:, :N_HEADS])
            xs = _oproj(xs, [o_s], [w_o], tm=nb)
            outs["k_c_s"].append(ksf.reshape(nb, 1, N_HEADS, HEAD_DIM))
            outs["v_c_s"].append(vsf.reshape(nb, 1, N_HEADS, HEAD_DIM))
            outs["f_c_s"].append(lfs[:, :N_HEADS].reshape(nb, 1, N_HEADS))

        gf = g_ffn[l].reshape(1, d)
        wu, wd = w_up[l].astype(BF16), w_down[l].astype(BF16)
        cb = conv_b[l].reshape(1, -1)
        yp, conv_p = _ffn_prompt(xp.reshape(b, t, d), gf, wu, conv_w[l], cb, wd, tm=tf, cw=cw)
        xp = yp.reshape(m, d)
        xs, conv_s = _ffn_sample(xs, gf, wu, conv_w[l], cb, wd, state_conv[l], cw=cw)
        outs["conv_p"].append(conv_p)
        outs["conv_s"].append(conv_s)

    gfin = g_final.reshape(1, d)
    y_prompt = _final_norm(xp, gfin, tm=tm).reshape(b, t, d)
    y_sample = _final_norm(xs, gfin, tm=nb).reshape(nb, 1, d)
    st = lambda key: jnp.stack(outs[key])

    def kv(key):
        a = kv_t[key]
        return a.reshape(a.shape[0], b, N_HEADS, HEAD_DIM, t).transpose(0, 1, 4, 2, 3)

    return (y_prompt, y_sample, kv("k_ab_p"), kv("v_ab_p"), st("k_ab_s"), st("v_ab_s"),
            kv("k_c_p"), kv("v_c_p"), st("f_c_p"), st("k_c_s"), st("v_c_s"), st("f_c_s"),
            st("conv_p"), st("conv_s"))
```
